```python
import math
import jax, jax.numpy as jnp
from jax import lax
import numpy as np

D_MODEL = 1024
BATCH = 4
SEQ = 4096
DEPTH = 2
DEC_BATCH = 32
DEC_SEQ = 8
PAST_LEN = 16384
PAGE_SIZE = 128

H_A = 8
DK_A = 128
DV_A = 128
W_A = H_A * DV_A
H_B = 4
DK_B = 128
DV_B = 256
W_B = H_B * DV_B
H_C = 8
DH_C = 128
W_C = H_C * DH_C
N_BRANCH = 3
CHUNK = 64
Q_BLOCK = 128
ROPE_BASE = 10000.0
RET_DECAY_BASE = 5.0
NORM_EPS = 1e-6
FOX_F_BIAS_INIT = 2.0
MASK_VALUE = -1e30
COL_SIZES = (H_A * DK_A, H_A * DK_A, W_A, W_A,
             H_B * DK_B, H_B * DK_B, W_B, W_B,
             W_C, W_C, W_C, H_C, W_C,
             N_BRANCH * D_MODEL)
D_IN = sum(COL_SIZES)

kernel_name = 'hybrid_hgrn2_retnet_fox_adaln_step'


def _rmsnorm(x, g):
    xf = x.astype(jnp.float32)
    y = xf * lax.rsqrt(jnp.mean(xf * xf, axis=-1, keepdims=True) + NORM_EPS)
    return (y * g.astype(jnp.float32)).astype(x.dtype)


def _rope(x, pos):
    half = x.shape[-1] // 2
    inv = ROPE_BASE ** (-jnp.arange(half, dtype=jnp.float32) / half)
    ang = pos.astype(jnp.float32)[:, None] * inv[None, :]
    cos = jnp.cos(ang)[None, :, None, :]
    sin = jnp.sin(ang)[None, :, None, :]
    xf = x.astype(jnp.float32)
    x1, x2 = xf[..., :half], xf[..., half:]
    return jnp.concatenate([x1 * cos - x2 * sin, x1 * sin + x2 * cos], axis=-1).astype(x.dtype)


def _to_chunks(t, n, c):
    b, _, h, d = t.shape
    return t.astype(jnp.float32).reshape(b, n, c, h, d).transpose(1, 0, 3, 2, 4)


def _from_chunks(o):
    n, b, h, c, d = o.shape
    return o.transpose(1, 0, 3, 2, 4).reshape(b, n * c, h, d)


def _hgrn2_chunked(q, k, v, logf, s0):
    L = q.shape[1]
    c = math.gcd(L, CHUNK)
    n = L // c
    causal = jnp.tril(jnp.ones((c, c), dtype=bool))[:, :, None]

    def step(S, xs):
        qi, ki, vi, gi = xs
        b = jnp.cumsum(gi, axis=2)
        diff = b[:, :, :, None, :] - b[:, :, None, :, :]
        decay = jnp.where(causal, jnp.exp(jnp.where(causal, diff, 0.0)), 0.0)
        a = jnp.einsum('bhtd,bhsd,bhtsd->bhts', qi, ki, decay)
        o = (jnp.einsum('bhts,bhsv->bhtv', a, vi)
             + jnp.einsum('bhtd,bhdv->bhtv', qi * jnp.exp(b), S))
        b_end = b[:, :, -1:, :]
        S = (jnp.exp(b_end[:, :, 0, :, None]) * S
             + jnp.einsum('bhsd,bhsv->bhdv', ki * jnp.exp(b_end - b), vi))
        return S, o

    xs = (_to_chunks(q, n, c), _to_chunks(k, n, c), _to_chunks(v, n, c), _to_chunks(logf, n, c))
    S, o = lax.scan(step, s0.astype(jnp.float32), xs)
    return _from_chunks(o).astype(v.dtype), S.astype(s0.dtype)


def _retention_chunked(q, k, v, log_gamma, s0):
    L = q.shape[1]
    c = math.gcd(L, CHUNK)
    n = L // c
    t = jnp.arange(c, dtype=jnp.float32)
    rel = t[:, None] - t[None, :]
    lg = log_gamma[:, None, None]
    intra = jnp.where(rel >= 0, jnp.exp(jnp.maximum(rel, 0.0) * lg), 0.0)
    q_dec = jnp.exp((t + 1.0) * log_gamma[:, None])[..., None]
    k_dec = jnp.exp((c - 1.0 - t) * log_gamma[:, None])[..., None]
    chunk_dec = jnp.exp(c * log_gamma)[:, None, None]

    def step(S, xs):
        qi, ki, vi = xs
        a = jnp.einsum('bhtd,bhsd->bhts', qi, ki) * intra
        o = (jnp.einsum('bhts,bhsv->bhtv', a, vi)
             + jnp.einsum('bhtd,bhdv->bhtv', qi * q_dec, S))
        S = chunk_dec * S + jnp.einsum('bhsd,bhsv->bhdv', ki * k_dec, vi)
        return S, o

    xs = (_to_chunks(q, n, c), _to_chunks(k, n, c), _to_chunks(v, n, c))
    S, o = lax.scan(step, s0.astype(jnp.float32), xs)
    return _from_chunks(o).astype(v.dtype), S.astype(s0.dtype)


def _fox_logits(q, k, fq, fk, mask):
    s = jnp.einsum('bqhd,bkhd->bhqk', q, k).astype(jnp.float32) * (q.shape[-1] ** -0.5)
    s = s + fq.transpose(0, 2, 1)[:, :, :, None] - fk.transpose(0, 2, 1)[:, :, None, :]
    if mask is not None:
        s = jnp.where(mask, s, MASK_VALUE)
    return s


def _fox_prompt(q, k, v, logf):
    B, L, H, D = q.shape
    qb = math.gcd(L, Q_BLOCK)
    nb = L // qb
    F = jnp.cumsum(logf.astype(jnp.float32), axis=1)
    kpos = jnp.arange(L)

    def block(xs):
        qi, fi, pi = xs
        s = _fox_logits(qi, k, fi, F, pi[:, None] >= kpos[None, :])
        p = jax.nn.softmax(s, axis=-1).astype(v.dtype)
        return jnp.einsum('bhqk,bkhd->bqhd', p, v)

    xs = (q.reshape(B, nb, qb, H, D).transpose(1, 0, 2, 3, 4),
          F.reshape(B, nb, qb, H).transpose(1, 0, 2, 3),
          kpos.reshape(nb, qb))
    o = lax.map(block, xs)
    return o.transpose(1, 0, 2, 3, 4).reshape(B, L, H, D)


def _fox_sample(q, k, v, logf, k_past, v_past, logf_past):
    P = k_past.shape[1]
    L = q.shape[1]
    F = jnp.cumsum(jnp.concatenate([logf_past.astype(jnp.float32), logf.astype(jnp.float32)], axis=1), axis=1)
    f_past, f_new = F[:, :P], F[:, P:]
    causal = jnp.tril(jnp.ones((L, L), dtype=bool))
    s = jnp.concatenate([_fox_logits(q, k_past, f_new, f_past, None),
                         _fox_logits(q, k, f_new, f_new, causal)], axis=-1)
    p = jax.nn.softmax(s, axis=-1).astype(v.dtype)
    return (jnp.einsum('bhqk,bkhd->bqhd', p[..., :P], v_past)
            + jnp.einsum('bhqk,bkhd->bqhd', p[..., P:], v))


def _mixer_layer(x, c, pos, lb, w_ada, b_ada, g_norm, w_in, b_fox_f, g_hgrn, g_ret,
                 w_proj_a, w_proj_b, w_proj_c, w_out, s_hgrn, s_ret, past):
    B, L, _ = x.shape
    shift, scale, gate = jnp.split(c @ w_ada + b_ada, 3, axis=-1)
    u = _rmsnorm(x, g_norm) * (1.0 + scale[:, None]) + shift[:, None]
    splits = np.cumsum(COL_SIZES)[:-1].tolist()
    (a_f, a_q, a_i, a_z, b_q, b_k, b_v, b_z,
     c_q, c_k, c_v, c_f, c_z, m_g) = jnp.split(u @ w_in, splits, axis=-1)

    fl = a_f.astype(jnp.float32).reshape(B, L, H_A, DK_A)
    lbh = lb.reshape(H_A, DK_A)
    logf_a = jnp.log(lbh + (1.0 - lbh) * jax.nn.sigmoid(fl))
    k_a = (1.0 - lbh) * jax.nn.sigmoid(-fl)
    q_a = jax.nn.silu(a_q).reshape(B, L, H_A, DK_A) * (DK_A ** -0.5)
    o_a, s_hgrn_new = _hgrn2_chunked(q_a, k_a, a_i.reshape(B, L, H_A, DV_A), logf_a, s_hgrn)
    y_a = _rmsnorm(o_a, g_hgrn).reshape(B, L, W_A) * jax.nn.silu(a_z)

    q_b = _rope(b_q.reshape(B, L, H_B, DK_B), pos) * (DK_B ** -0.5)
    k_b = _rope(b_k.reshape(B, L, H_B, DK_B), pos)
    log_gamma = jnp.log1p(-jnp.power(2.0, -RET_DECAY_BASE - jnp.arange(H_B, dtype=jnp.float32)))
    o_b, s_ret_new = _retention_chunked(q_b, k_b, b_v.reshape(B, L, H_B, DV_B), log_gamma, s_ret)
    y_b = _rmsnorm(o_b, g_ret).reshape(B, L, W_B) * jax.nn.silu(b_z)

    q_c = c_q.reshape(B, L, H_C, DH_C)
    k_c = c_k.reshape(B, L, H_C, DH_C)
    v_c = c_v.reshape(B, L, H_C, DH_C)
    logf_c = jax.nn.log_sigmoid((c_f + b_fox_f).astype(jnp.float32))
    if past is None:
        o_c = _fox_prompt(q_c, k_c, v_c, logf_c)
    else:
        o_c = _fox_sample(q_c, k_c, v_c, logf_c, *past)
    y_c = o_c.reshape(B, L, W_C) * jax.nn.silu(c_z)

    g = jax.nn.sigmoid(m_g).reshape(B, L, N_BRANCH, D_MODEL)
    merged = (g[:, :, 0] * (y_a @ w_proj_a) + g[:, :, 1] * (y_b @ w_proj_b)
              + g[:, :, 2] * (y_c @ w_proj_c))
    x = x + gate[:, None] * (merged @ w_out)
    return x, s_hgrn_new, s_ret_new, k_c, v_c, logf_c


def setup_inputs(seed: int = 0) -> dict:
    key = jax.random.key(seed)
    ks = jax.random.split(key, 24)
    n_pages = PAST_LEN // PAGE_SIZE
    n_used = DEC_BATCH * n_pages
    n_phys = n_used + n_used // 4
    nrm = jax.random.normal
    f32 = jnp.float32
    return {
        'x_prompt': nrm(ks[0], (BATCH, SEQ, D_MODEL), f32),
        'x_sample': nrm(ks[1], (DEC_BATCH, DEC_SEQ, D_MODEL), f32),
        'c_prompt': nrm(ks[2], (BATCH, D_MODEL), f32),
        'c_sample': nrm(ks[3], (DEC_BATCH, D_MODEL), f32),
        'cache_k': nrm(ks[4], (n_phys, DEPTH, PAGE_SIZE, H_C, DH_C), f32),
        'cache_v': nrm(ks[5], (n_phys, DEPTH, PAGE_SIZE, H_C, DH_C), f32),
        'cache_logf': jax.nn.log_sigmoid(FOX_F_BIAS_INIT + nrm(ks[6], (n_phys, DEPTH, PAGE_SIZE, H_C), f32)),
        'state_hgrn': 0.5 * nrm(ks[7], (DEC_BATCH, DEPTH, H_A, DK_A, DV_A), f32),
        'state_ret': 0.5 * nrm(ks[8], (DEC_BATCH, DEPTH, H_B, DK_B, DV_B), f32),
        'page_table': jax.random.permutation(ks[9], n_phys)[:n_used].reshape(DEC_BATCH, n_pages).astype(jnp.int32),
        'w_ada': 0.2 * D_MODEL ** -0.5 * nrm(ks[10], (DEPTH, D_MODEL, 3 * D_MODEL), f32),
        'b_ada': 0.01 * nrm(ks[11], (DEPTH, 3 * D_MODEL), f32),
        'g_norm': 1.0 + 0.01 * nrm(ks[12], (DEPTH, D_MODEL), f32),
        'w_in': D_MODEL ** -0.5 * nrm(ks[13], (DEPTH, D_MODEL, D_IN), f32),
        'b_fox_f': FOX_F_BIAS_INIT + 0.1 * nrm(ks[14], (DEPTH, H_C), f32),
        'lb_logits': 0.5 * nrm(ks[15], (DEPTH, H_A * DK_A), f32),
        'g_hgrn': 1.0 + 0.01 * nrm(ks[16], (DEPTH, DV_A), f32),
        'g_ret': 1.0 + 0.01 * nrm(ks[17], (DEPTH, DV_B), f32),
        'w_proj_a': W_A ** -0.5 * nrm(ks[18], (DEPTH, W_A, D_MODEL), f32),
        'w_proj_b': W_B ** -0.5 * nrm(ks[19], (DEPTH, W_B, D_MODEL), f32),
        'w_proj_c': W_C ** -0.5 * nrm(ks[20], (DEPTH, W_C, D_MODEL), f32),
        'w_out': D_MODEL ** -0.5 * nrm(ks[21], (DEPTH, D_MODEL, D_MODEL), f32),
        'g_final': 1.0 + 0.01 * nrm(ks[22], (D_MODEL,), f32),
    }


def reference(x_prompt, x_sample, c_prompt, c_sample, cache_k, cache_v, cache_logf,
              state_hgrn, state_ret, page_table, w_ada, b_ada, g_norm, w_in, b_fox_f,
              lb_logits, g_hgrn, g_ret, w_proj_a, w_proj_b, w_proj_c, w_out, g_final):
    n_pages = PAST_LEN // PAGE_SIZE
    past_len = n_pages * PAGE_SIZE
    pos_p = jnp.arange(SEQ, dtype=jnp.int32)
    pos_s = PAST_LEN + jnp.arange(DEC_SEQ, dtype=jnp.int32)
    sm = jax.nn.softmax(lb_logits.astype(jnp.float32), axis=0)
    lower_bounds = jnp.cumsum(sm, axis=0) - sm[0]
    zero_hgrn = jnp.zeros((BATCH, H_A, DK_A, DV_A), jnp.float32)
    zero_ret = jnp.zeros((BATCH, H_B, DK_B, DV_B), jnp.float32)
    xp, xs = x_prompt, x_sample
    kp, vp, fp, hp, rp = [], [], [], [], []
    ksm, vsm, fsm, hsm, rsm = [], [], [], [], []
    for l in range(DEPTH):
        w = (lower_bounds[l], w_ada[l], b_ada[l], g_norm[l], w_in[l], b_fox_f[l], g_hgrn[l],
             g_ret[l], w_proj_a[l], w_proj_b[l], w_proj_c[l], w_out[l])
        xp, h, r, k, v, f = _mixer_layer(xp, c_prompt, pos_p, *w, zero_hgrn, zero_ret, None)
        kp.append(k); vp.append(v); fp.append(f); hp.append(h); rp.append(r)
        k_past = cache_k[page_table, l].reshape(DEC_BATCH, past_len, H_C, DH_C)
        v_past = cache_v[page_table, l].reshape(DEC_BATCH, past_len, H_C, DH_C)
        f_past = cache_logf[page_table, l].reshape(DEC_BATCH, past_len, H_C)
        xs, h, r, k, v, f = _mixer_layer(xs, c_sample, pos_s, *w, state_hgrn[:, l], state_ret[:, l],
                                         (k_past, v_past, f_past))
        ksm.append(k); vsm.append(v); fsm.append(f); hsm.append(h); rsm.append(r)
    y_prompt = _rmsnorm(xp, g_final)
    y_sample = _rmsnorm(xs, g_final)
    return (y_prompt, y_sample,
            jnp.stack(kp, axis=1), jnp.stack(vp, axis=1), jnp.stack(fp, axis=1),
            jnp.stack(hp, axis=1), jnp.stack(rp, axis=1),
            jnp.stack(ksm, axis=1), jnp.stack(vsm, axis=1), jnp.stack(fsm, axis=1),
            jnp.stack(hsm, axis=1), jnp.stack(rsm, axis=1))
```

```python
import functools
import math

import numpy as np
import jax
import jax.numpy as jnp
from jax import lax
from jax.experimental import pallas as pl
from jax.experimental.pallas import tpu as pltpu

F32 = jnp.float32
BF16 = jnp.bfloat16

H_A, DK_A, DV_A = 8, 128, 128
H_B, DK_B, DV_B = 4, 128, 256
H_C, DH_C = 8, 128
W_A = H_A * DV_A
W_B = H_B * DV_B
W_C = H_C * DH_C
N_BRANCH = 3
ROPE_BASE = 10000.0
RET_DECAY_BASE = 5.0
NORM_EPS = 1e-6
MASK_VALUE = -1e30

LANES = 128
HGRN_CHUNK = 128
RET_CHUNK = 256
FOX_BLOCK = 512
VMEM_LIMIT = 56 * 1024 * 1024


def _cparams(sem):
    return pltpu.CompilerParams(dimension_semantics=sem, vmem_limit_bytes=VMEM_LIMIT)


def _dot(a, b):
    return jnp.dot(a, b, preferred_element_type=F32)


def _dot_nt(a, b):
    return lax.dot_general(a, b, (((1,), (1,)), ((), ())), preferred_element_type=F32)


def _split3(x):
    hi = x.astype(BF16)
    r = x - hi.astype(F32)
    mid = r.astype(BF16)
    lo = (r - mid.astype(F32)).astype(BF16)
    return hi, mid, lo


def _dot_exact_lhs(t, x):
    hi, mid, lo = _split3(x)
    return _dot(t, hi) + _dot(t, mid) + _dot(t, lo)


def _dot_exact_rhs(x, t):
    hi, mid, lo = _split3(x)
    return _dot(hi, t) + _dot(mid, t) + _dot(lo, t)


def _sigmoid(x):
    return 1.0 / (1.0 + jnp.exp(-x))


def _silu(x):
    return x * _sigmoid(x)


def _log_sigmoid(x):
    return jnp.minimum(x, 0.0) - jnp.log(1.0 + jnp.exp(-jnp.abs(x)))


def _ada_body(c_ref, w_ref, b_ref, o_ref):
    c = c_ref[...]
    w = w_ref[0]
    c_hi = c.astype(BF16)
    c_lo = (c - c_hi.astype(F32)).astype(BF16)
    w_hi = w.astype(BF16)
    w_lo = (w - w_hi.astype(F32)).astype(BF16)
    o_ref[0] = _dot(c_hi, w_hi) + _dot(c_hi, w_lo) + _dot(c_lo, w_hi) + b_ref[0]


def _ada_mod(c_all, w_ada, b_ada):
    depth, d, d3 = w_ada.shape
    rows = c_all.shape[0]
    return pl.pallas_call(
        _ada_body,
        grid=(depth, d3 // d),
        in_specs=[pl.BlockSpec((rows, d), lambda l, j: (0, 0)),
                  pl.BlockSpec((1, d, d), lambda l, j: (l, 0, j)),
                  pl.BlockSpec((1, 1, d), lambda l, j: (l, 0, j))],
        out_specs=pl.BlockSpec((1, rows, d), lambda l, j: (l, 0, j)),
        out_shape=jax.ShapeDtypeStruct((depth, rows, d3), F32),
        compiler_params=_cparams(("arbitrary", "arbitrary")),
        name="ada_mod",
    )(c_all, w_ada, b_ada.reshape(depth, 1, d3))


def _mod_spec(mod, tm, seq):
    if mod.ndim == 3:
        return pl.BlockSpec((None, 1, mod.shape[-1]), lambda i: ((i * tm) // seq, 0, 0))
    return pl.BlockSpec((tm, mod.shape[-1]), lambda i: (i, 0))


def _mod_arrays(mod_l, batch, seq, tm, d):
    parts = [mod_l[:, k * d:(k + 1) * d] for k in range(3)]
    if seq % tm == 0:
        return [p.reshape(batch, 1, d) for p in parts]
    return [jnp.broadcast_to(p[:, None, :], (batch, seq, d)).reshape(batch * seq, d) for p in parts]


def _norm_body(x_ref, g_ref, sh_ref, sc_ref, u_ref):
    x = x_ref[...]
    ms = jnp.mean(x * x, axis=-1, keepdims=True)
    y = x * lax.rsqrt(ms + NORM_EPS) * g_ref[...]
    u_ref[...] = (y * (1.0 + sc_ref[...]) + sh_ref[...]).astype(BF16)


def _norm_mod(x2, g, shift, scale, tm, seq):
    m, d = x2.shape
    return pl.pallas_call(
        _norm_body,
        grid=(m // tm,),
        in_specs=[pl.BlockSpec((tm, d), lambda i: (i, 0)),
                  pl.BlockSpec((1, d), lambda i: (0, 0)),
                  _mod_spec(shift, tm, seq), _mod_spec(scale, tm, seq)],
        out_specs=pl.BlockSpec((tm, d), lambda i: (i, 0)),
        out_shape=jax.ShapeDtypeStruct((m, d), BF16),
        compiler_params=_cparams(("parallel",)),
        name="norm_mod",
    )(x2, g.reshape(1, d), shift, scale)


def _proj_hgrn_body(layer, u_ref, wf_ref, wq_ref, wi_ref, wz_ref, lbl_ref,
                    logf_ref, k_ref, q_ref, v_ref, z_ref):
    u = u_ref[...]
    lbl = lbl_ref[...]
    e = jnp.exp(lbl - jnp.max(lbl, axis=0, keepdims=True))
    sm = e / jnp.sum(e, axis=0, keepdims=True)
    lb = jnp.zeros_like(sm[0:1])
    for r in range(1, layer + 1):
        lb = lb + sm[r:r + 1]
    fl = _dot(u, wf_ref[...])
    logf_ref[...] = jnp.log(lb + (1.0 - lb) * _sigmoid(fl))
    k_ref[...] = (1.0 - lb) * _sigmoid(-fl)
    q_ref[...] = _silu(_dot(u, wq_ref[...])) * (DK_A ** -0.5)
    v_ref[...] = _dot(u, wi_ref[...]).astype(BF16)
    z_ref[...] = _silu(_dot(u, wz_ref[...]))


def _proj_hgrn(u, wf, wq, wi, wz, lb_logits, layer, tm, tn=256):
    m, d = u.shape
    n = wf.shape[1]
    depth = lb_logits.shape[0]
    wspec = pl.BlockSpec((d, tn), lambda j, i: (0, j))
    ospec = pl.BlockSpec((tm, tn), lambda j, i: (i, j))
    return pl.pallas_call(
        functools.partial(_proj_hgrn_body, layer),
        grid=(n // tn, m // tm),
        in_specs=[pl.BlockSpec((tm, d), lambda j, i: (i, 0)), wspec, wspec, wspec, wspec,
                  pl.BlockSpec((depth, tn), lambda j, i: (0, j))],
        out_specs=[ospec] * 5,
        out_shape=[jax.ShapeDtypeStruct((m, n), F32), jax.ShapeDtypeStruct((m, n), F32),
                   jax.ShapeDtypeStruct((m, n), F32), jax.ShapeDtypeStruct((m, n), BF16),
                   jax.ShapeDtypeStruct((m, n), F32)],
        compiler_params=_cparams(("parallel", "parallel")),
        name="proj_hgrn",
    )(u, wf, wq, wi, wz, lb_logits)


def _rope(x, cos, sin_signed):
    return x * cos + pltpu.roll(x, DK_B // 2, axis=1) * sin_signed


def _proj_ret_body(u_ref, wq_ref, wk_ref, wv_ref, wz_ref, cos_ref, sin_ref,
                   q_ref, k_ref, v_ref, z_ref):
    u = u_ref[...]
    cos = cos_ref[...]
    sin = sin_ref[...]
    q_ref[...] = _rope(_dot(u, wq_ref[...]), cos, sin) * (DK_B ** -0.5)
    k_ref[...] = _rope(_dot(u, wk_ref[...]), cos, sin)
    v_ref[...] = _dot(u, wv_ref[...]).astype(BF16)
    z_ref[...] = _silu(_dot(u, wz_ref[...]))


def _proj_ret(u, wq, wk, wv, wz, cos, sin, tm):
    m, d = u.shape
    n_rope_tiles = cos.shape[0] // tm
    qk_w = pl.BlockSpec((d, DK_B), lambda h, i: (0, h))
    vz_w = pl.BlockSpec((d, DV_B), lambda h, i: (0, h))
    rope_spec = pl.BlockSpec((tm, DK_B), lambda h, i: (i % n_rope_tiles, 0))
    qk_o = pl.BlockSpec((tm, DK_B), lambda h, i: (i, h))
    vz_o = pl.BlockSpec((tm, DV_B), lambda h, i: (i, h))
    return pl.pallas_call(
        _proj_ret_body,
        grid=(H_B, m // tm),
        in_specs=[pl.BlockSpec((tm, d), lambda h, i: (i, 0)), qk_w, qk_w, vz_w, vz_w,
                  rope_spec, rope_spec],
        out_specs=[qk_o, qk_o, vz_o, vz_o],
        out_shape=[jax.ShapeDtypeStruct((m, H_B * DK_B), F32), jax.ShapeDtypeStruct((m, H_B * DK_B), F32),
                   jax.ShapeDtypeStruct((m, W_B), BF16), jax.ShapeDtypeStruct((m, W_B), F32)],
        compiler_params=_cparams(("parallel", "parallel")),
        name="proj_ret",
    )(u, wq, wk, wv, wz, cos, sin)


def _proj_fox_body(u_ref, wq_ref, wk_ref, wv_ref, wz_ref,
                   q16_ref, k32_ref, k16_ref, v32_ref, v16_ref, z_ref):
    u = u_ref[...]
    q16_ref[...] = _dot(u, wq_ref[...]).astype(BF16)
    k = _dot(u, wk_ref[...])
    k32_ref[...] = k
    k16_ref[...] = k.astype(BF16)
    v = _dot(u, wv_ref[...])
    v32_ref[...] = v
    v16_ref[...] = v.astype(BF16)
    z_ref[...] = _silu(_dot(u, wz_ref[...]))


def _proj_fox(u, wq, wk, wv, wz, tm, tn=256):
    m, d = u.shape
    n = wq.shape[1]
    wspec = pl.BlockSpec((d, tn), lambda j, i: (0, j))
    ospec = pl.BlockSpec((tm, tn), lambda j, i: (i, j))
    shp = lambda dt: jax.ShapeDtypeStruct((m, n), dt)
    return pl.pallas_call(
        _proj_fox_body,
        grid=(n // tn, m // tm),
        in_specs=[pl.BlockSpec((tm, d), lambda j, i: (i, 0)), wspec, wspec, wspec, wspec],
        out_specs=[ospec] * 6,
        out_shape=[shp(BF16), shp(F32), shp(BF16), shp(F32), shp(BF16), shp(F32)],
        compiler_params=_cparams(("parallel", "parallel")),
        name="proj_fox",
    )(u, wq, wk, wv, wz)


def _proj_foxf_body(u_ref, whi_ref, wlo_ref, b_ref, o_ref):
    u = u_ref[...]
    cf = _dot(u, whi_ref[...]) + _dot(u, wlo_ref[...]) + b_ref[...]
    o_ref[...] = _log_sigmoid(cf)[:, :H_C]


def _proj_foxf(u, wf, bf, tm):
    m, d = u.shape
    wpad = jnp.pad(wf, ((0, 0), (0, LANES - H_C)))
    whi = wpad.astype(BF16)
    wlo = (wpad - whi.astype(F32)).astype(BF16)
    bpad = jnp.pad(bf.reshape(1, H_C), ((0, 0), (0, LANES - H_C)))
    wspec = pl.BlockSpec((d, LANES), lambda i: (0, 0))
    return pl.pallas_call(
        _proj_foxf_body,
        grid=(m // tm,),
        in_specs=[pl.BlockSpec((tm, d), lambda i: (i, 0)), wspec, wspec,
                  pl.BlockSpec((1, LANES), lambda i: (0, 0))],
        out_specs=pl.BlockSpec((tm, H_C), lambda i: (i, 0)),
        out_shape=jax.ShapeDtypeStruct((m, H_C), F32),
        compiler_params=_cparams(("parallel",)),
        name="proj_foxf",
    )(u, whi, wlo, bpad)


def _proj_gate_body(u_ref, w_ref, o_ref):
    o_ref[...] = _sigmoid(_dot(u_ref[...], w_ref[...]))


def _proj_gate(u, wg, tm, tn=512):
    m, d = u.shape
    n = wg.shape[1]
    return pl.pallas_call(
        _proj_gate_body,
        grid=(n // tn, m // tm),
        in_specs=[pl.BlockSpec((tm, d), lambda j, i: (i, 0)),
                  pl.BlockSpec((d, tn), lambda j, i: (0, j))],
        out_specs=pl.BlockSpec((tm, tn), lambda j, i: (i, j)),
        out_shape=jax.ShapeDtypeStruct((m, n), F32),
        compiler_params=_cparams(("parallel", "parallel")),
        name="proj_gate",
    )(u, wg)


def _hgrn_levels(c):
    return [c >> (i + 1) for i in range(int(math.log2(c)))]


def _hgrn_tables(c):
    t = np.arange(c)[:, None]
    u = np.arange(c)[None, :]
    mats, masks = [], []
    for w in _hgrn_levels(c):
        r = (t // (2 * w)) * (2 * w) + w
        upper = (t % (2 * w)) >= w
        mats.append(np.where(upper, (u >= r) & (u <= t), (u > t) & (u < r)))
        masks.append((t // (2 * w)) == (u // (2 * w)))
    mats.append(u <= t)
    mats.append(u > t)
    masks.append(t == u)
    return (jnp.asarray(np.concatenate(mats, 0).astype(np.float32), BF16),
            jnp.asarray(np.stack(masks, 0).astype(np.float32), F32))


def _hgrn_body(c, nc, logf_ref, k_ref, q_ref, v_ref, z_ref, g_ref, s0_ref, t_ref, mask_ref,
               y_ref, sout_ref, st_ref):
    ci = pl.program_id(2)

    @pl.when(ci == 0)
    def _():
        st_ref[...] = s0_ref[0, 0].T

    levels = _hgrn_levels(c)
    nl = len(levels)
    e_all = jnp.exp(_dot_exact_lhs(t_ref[...], logf_ref[...]))
    q = q_ref[...]
    k = k_ref[...]
    v = v_ref[...]
    row = lax.broadcasted_iota(jnp.int32, (c, DK_A), 0)
    a = mask_ref[nl] * _dot_nt(q.astype(BF16), k.astype(BF16))
    for li, w in enumerate(levels):
        e_w = e_all[li * c:(li + 1) * c]
        upper = (row & w) != 0
        q_w = jnp.where(upper, q * e_w, 0.0).astype(BF16)
        k_w = jnp.where(upper, 0.0, k * e_w).astype(BF16)
        a = a + mask_ref[li] * _dot_nt(q_w, k_w)
    e_b = e_all[nl * c:(nl + 1) * c]
    e_end = e_all[(nl + 1) * c:(nl + 2) * c]
    st = st_ref[...]
    o = _dot(a.astype(BF16), v) + _dot_nt((q * e_b).astype(BF16), st.astype(BF16))
    v_t = v.astype(F32).T.astype(BF16)
    st_ref[...] = st * e_b[c - 1:c, :] + _dot(v_t, (k * e_end).astype(BF16))

    ms = jnp.mean(o * o, axis=-1, keepdims=True)
    y_ref[...] = (o * lax.rsqrt(ms + NORM_EPS) * g_ref[...] * z_ref[...]).astype(BF16)

    @pl.when(ci == nc - 1)
    def _():
        sout_ref[0, 0] = st_ref[...].T


def _hgrn_scan(logf, k, q, v, z, g, s0, batch, seq):
    c = HGRN_CHUNK
    nc = seq // c
    m = batch * seq
    tmat, masks = _hgrn_tables(c)
    tok = pl.BlockSpec((c, DK_A), lambda b, h, ci: (b * nc + ci, h))
    st_spec = pl.BlockSpec((1, 1, DK_A, DV_A), lambda b, h, ci: (b, h, 0, 0))
    return pl.pallas_call(
        functools.partial(_hgrn_body, c, nc),
        grid=(batch, H_A, nc),
        in_specs=[tok, tok, tok, tok, tok,
                  pl.BlockSpec((1, DV_A), lambda b, h, ci: (0, 0)),
                  st_spec,
                  pl.BlockSpec(tmat.shape, lambda b, h, ci: (0, 0)),
                  pl.BlockSpec(masks.shape, lambda b, h, ci: (0, 0, 0))],
        out_specs=[tok, st_spec],
        out_shape=[jax.ShapeDtypeStruct((m, W_A), BF16),
                   jax.ShapeDtypeStruct((batch, H_A, DK_A, DV_A), F32)],
        scratch_shapes=[pltpu.VMEM((DV_A, DK_A), F32)],
        compiler_params=_cparams(("parallel", "parallel", "arbitrary")),
        name="hgrn_scan",
    )(logf, k, q, v, z, g.reshape(1, DV_A), s0, tmat, masks)


def _ret_body(c, c_eff, nc, lg_ref, q_ref, k_ref, v_ref, z_ref, g_ref, s0_ref,
              y_ref, sout_ref, s_ref):
    h = pl.program_id(1)
    ci = pl.program_id(2)

    @pl.when(ci == 0)
    def _():
        s_ref[...] = s0_ref[0, 0]

    lg = lg_ref[h]
    q = q_ref[...]
    k = k_ref[...]
    v = v_ref[...]
    t = lax.broadcasted_iota(jnp.int32, (c, DK_B), 0).astype(F32)
    q_dec = jnp.exp((t + 1.0) * lg)
    k_dec = jnp.where(t < c_eff, jnp.exp(jnp.maximum(c_eff - 1.0 - t, 0.0) * lg), 0.0)
    rel = (lax.broadcasted_iota(jnp.int32, (c, c), 0)
           - lax.broadcasted_iota(jnp.int32, (c, c), 1)).astype(F32)
    intra = jnp.where(rel >= 0.0, jnp.exp(jnp.maximum(rel, 0.0) * lg), 0.0)
    a = _dot_nt(q.astype(BF16), k.astype(BF16)) * intra
    s = s_ref[...]
    o = _dot(a.astype(BF16), v) + _dot((q * q_dec).astype(BF16), s.astype(BF16))
    kd_t = (k * k_dec).T.astype(BF16)
    chunk_dec = jnp.exp(jnp.full((1, DV_B), c_eff, F32) * lg)
    s_ref[...] = chunk_dec * s + _dot(kd_t, v)

    ms = jnp.mean(o * o, axis=-1, keepdims=True)
    y_ref[...] = (o * lax.rsqrt(ms + NORM_EPS) * g_ref[...] * z_ref[...]).astype(BF16)

    @pl.when(ci == nc - 1)
    def _():
        sout_ref[0, 0] = s_ref[...]


def _ret_scan(q, k, v, z, g, s0, batch, seq, c, c_eff):
    nc = seq // c
    m = batch * seq
    log_gamma = jnp.log1p(-jnp.power(2.0, -RET_DECAY_BASE - jnp.arange(H_B, dtype=F32)))
    qk = pl.BlockSpec((c, DK_B), lambda b, h, ci: (b * nc + ci, h))
    vz = pl.BlockSpec((c, DV_B), lambda b, h, ci: (b * nc + ci, h))
    st_spec = pl.BlockSpec((1, 1, DK_B, DV_B), lambda b, h, ci: (b, h, 0, 0))
    return pl.pallas_call(
        functools.partial(_ret_body, c, c_eff, nc),
        grid=(batch, H_B, nc),
        in_specs=[pl.BlockSpec(memory_space=pltpu.SMEM), qk, qk, vz, vz,
                  pl.BlockSpec((1, DV_B), lambda b, h, ci: (0, 0)), st_spec],
        out_specs=[vz, st_spec],
        out_shape=[jax.ShapeDtypeStruct((m, W_B), BF16),
                   jax.ShapeDtypeStruct((batch, H_B, DK_B, DV_B), F32)],
        scratch_shapes=[pltpu.VMEM((DK_B, DV_B), F32)],
        compiler_params=_cparams(("parallel", "parallel", "arbitrary")),
        name="ret_scan",
    )(log_gamma, q, k, v, z, g.reshape(1, DV_B), s0)


def _fcum_body(nblk, x_ref, o_ref):
    r = lax.broadcasted_iota(jnp.int32, (LANES, LANES), 0)
    cc = lax.broadcasted_iota(jnp.int32, (LANES, LANES), 1)
    tri = jnp.where(r <= cc, 1.0, 0.0).astype(BF16)
    carry = jnp.zeros((H_C, 1), F32)
    for blk in range(nblk):
        x = x_ref[0, :, blk * LANES:(blk + 1) * LANES]
        y = _dot_exact_rhs(x, tri) + carry
        o_ref[0, :, blk * LANES:(blk + 1) * LANES] = y
        carry = y[:, LANES - 1:LANES]


def _fcum(logf_t):
    b, h, seq = logf_t.shape
    spec = pl.BlockSpec((1, h, seq), lambda i: (i, 0, 0))
    return pl.pallas_call(
        functools.partial(_fcum_body, seq // LANES),
        grid=(b,),
        in_specs=[spec], out_specs=spec,
        out_shape=jax.ShapeDtypeStruct((b, h, seq), F32),
        compiler_params=_cparams(("parallel",)),
        name="fox_fcum",
    )(logf_t)


def _fox_prompt_body(tq, q_ref, k_ref, v_ref, fq_ref, fk_ref, z_ref, o_ref, fqc_ref):
    i = pl.program_id(2)
    scale = DH_C ** -0.5
    q = q_ref[...]
    fqc_ref[...] = jnp.broadcast_to(fq_ref[0, 0], (LANES, tq)).T
    fq = fqc_ref[:, 0:1]

    def step(j, carry, masked):
        m_prev, l_prev, acc = carry
        start = pl.multiple_of(j * tq, tq)
        kb = k_ref[pl.ds(start, tq), :]
        vb = v_ref[pl.ds(start, tq), :]
        fk = fk_ref[0, 0, :, pl.ds(start, tq)]
        s = _dot_nt(q, kb) * scale + fq - fk
        if masked:
            rr = lax.broadcasted_iota(jnp.int32, (tq, tq), 0)
            cc = lax.broadcasted_iota(jnp.int32, (tq, tq), 1)
            s = jnp.where(rr >= cc, s, MASK_VALUE)
        m_new = jnp.maximum(m_prev, jnp.max(s, axis=-1, keepdims=True))
        p = jnp.exp(s - m_new)
        alpha = jnp.exp(m_prev - m_new)
        l_new = alpha * l_prev + jnp.sum(p, axis=-1, keepdims=True)
        acc = alpha * acc + _dot(p.astype(BF16), vb)
        return m_new, l_new, acc

    init = (jnp.full((tq, 1), MASK_VALUE, F32), jnp.zeros((tq, 1), F32), jnp.zeros((tq, DH_C), F32))
    carry = lax.fori_loop(0, i, lambda j, cr: step(j, cr, False), init)
    _, l_fin, acc = step(i, carry, True)
    o_ref[...] = (acc / l_fin * z_ref[...]).astype(BF16)


def _fox_prompt(q16, k16, v16, f_row, z, batch, seq):
    tq = min(FOX_BLOCK, seq)
    nq = seq // tq
    m = batch * seq
    qspec = pl.BlockSpec((tq, DH_C), lambda b, h, i: (b * nq + i, h))
    kvspec = pl.BlockSpec((seq, DH_C), lambda b, h, i: (b, h))
    return pl.pallas_call(
        functools.partial(_fox_prompt_body, tq),
        grid=(batch, H_C, nq),
        in_specs=[qspec, kvspec, kvspec,
                  pl.BlockSpec((1, 1, 1, tq), lambda b, h, i: (b, h, 0, i)),
                  pl.BlockSpec((1, 1, 1, seq), lambda b, h, i: (b, h, 0, 0)),
                  qspec],
        out_specs=qspec,
        out_shape=jax.ShapeDtypeStruct((m, W_C), BF16),
        scratch_shapes=[pltpu.VMEM((tq, LANES), F32)],
        compiler_params=_cparams(("parallel", "parallel", "arbitrary")),
        name="fox_prompt",
    )(q16, k16, v16, f_row, f_row, z)


def _tri_f32(n, kind):
    r = lax.broadcasted_iota(jnp.int32, (n, n), 0)
    c = lax.broadcasted_iota(jnp.int32, (n, n), 1)
    cond = (c > r) if kind == "after" else (c <= r)
    return jnp.where(cond, 1.0, 0.0).astype(BF16)


def _past_bias_body(ps, pt_ref, x_ref, o_ref, carry_ref):
    p = pl.program_id(1)

    @pl.when(p == 0)
    def _():
        carry_ref[...] = jnp.zeros_like(carry_ref)

    x = x_ref[0, 0]
    suffix = _dot_exact_lhs(_tri_f32(ps, "after"), x)
    o_ref[0, 0] = suffix + carry_ref[...]
    carry_ref[...] = carry_ref[...] + jnp.sum(x, axis=0, keepdims=True)


def _past_bias(cache_logf, page_table, layer):
    nb, n_pages = page_table.shape
    ps = cache_logf.shape[2]
    grid_spec = pltpu.PrefetchScalarGridSpec(
        num_scalar_prefetch=1,
        grid=(nb, n_pages),
        in_specs=[pl.BlockSpec((1, 1, ps, H_C), lambda b, p, pt: (pt[b, n_pages - 1 - p], layer, 0, 0))],
        out_specs=pl.BlockSpec((1, 1, ps, H_C), lambda b, p, pt: (b, n_pages - 1 - p, 0, 0)),
        scratch_shapes=[pltpu.VMEM((1, H_C), F32)],
    )
    return pl.pallas_call(
        functools.partial(_past_bias_body, ps),
        grid_spec=grid_spec,
        out_shape=jax.ShapeDtypeStruct((nb, n_pages, ps, H_C), F32),
        compiler_params=_cparams(("parallel", "arbitrary")),
        name="fox_past_bias",
    )(page_table, cache_logf)


def _new_bias_body(n, x_ref, o_ref):
    o_ref[0] = -_dot_exact_lhs(_tri_f32(n, "upto"), x_ref[0])


def _new_bias(logf_new_pad):
    nb, n, h = logf_new_pad.shape
    spec = pl.BlockSpec((1, n, h), lambda b: (b, 0, 0))
    return pl.pallas_call(
        functools.partial(_new_bias_body, n),
        grid=(nb,),
        in_specs=[spec], out_specs=spec,
        out_shape=jax.ShapeDtypeStruct((nb, n, h), F32),
        compiler_params=_cparams(("parallel",)),
        name="fox_new_bias",
    )(logf_new_pad)


def _fox_decode_body(n_pages, ps, n_new, new_pad, pt_ref, q_ref, k_ref, v_ref, g_ref,
                     kn_ref, vn_ref, hn_ref, z_ref, o_ref, m_ref, l_ref, acc_ref, hm_ref):
    p = pl.program_id(1)
    scale = DH_C ** -0.5
    rows = H_C * n_new
    cols = ps * H_C

    @pl.when(p == 0)
    def _():
        m_ref[...] = jnp.full_like(m_ref, MASK_VALUE)
        l_ref[...] = jnp.zeros_like(l_ref)
        acc_ref[...] = jnp.zeros_like(acc_ref)
        rh = lax.broadcasted_iota(jnp.int32, (rows, cols), 0) // n_new
        ch = lax.broadcasted_iota(jnp.int32, (rows, cols), 1) % H_C
        hm_ref[...] = jnp.where(rh == ch, 0.0, MASK_VALUE)

    def update(s, vflat):
        m_prev = m_ref[:, 0:1]
        m_new = jnp.maximum(m_prev, jnp.max(s, axis=-1, keepdims=True))
        pr = jnp.exp(s - m_new)
        alpha = jnp.exp(m_prev - m_new)
        l_ref[...] = alpha * l_ref[...] + jnp.sum(pr, axis=-1, keepdims=True)
        acc_ref[...] = alpha * acc_ref[...] + _dot(pr.astype(BF16), vflat)
        m_ref[...] = jnp.broadcast_to(m_new, m_ref.shape)

    q = q_ref[0]

    @pl.when(p < n_pages)
    def _():
        kf = k_ref[0, 0].reshape(cols, DH_C).astype(BF16)
        vf = v_ref[0, 0].reshape(cols, DH_C).astype(BF16)
        s = _dot_nt(q, kf) * scale + g_ref[0, 0] + hm_ref[...]
        update(s, vf)

    @pl.when(p == n_pages)
    def _():
        ncol = new_pad * H_C
        s = _dot_nt(q, kn_ref[0]) * scale + hn_ref[0]
        rr = lax.broadcasted_iota(jnp.int32, (rows, ncol), 0)
        cc = lax.broadcasted_iota(jnp.int32, (rows, ncol), 1)
        tok = cc // H_C
        ok = ((rr // n_new) == (cc % H_C)) & (tok <= (rr % n_new)) & (tok < n_new)
        s = jnp.where(ok, s, MASK_VALUE)
        update(s, vn_ref[0])
        o_ref[0] = (acc_ref[...] / l_ref[:, 0:1] * z_ref[0]).astype(BF16)


def _fox_decode(q_rows, cache_k, cache_v, g_rows, k_new, v_new, hn_rows, z_rows, page_table, layer,
                n_new, new_pad):
    nb, n_pages = page_table.shape
    ps = cache_k.shape[2]
    rows = H_C * n_new
    cols = ps * H_C
    ncol = new_pad * H_C
    last = n_pages - 1
    page_spec = pl.BlockSpec((1, 1, ps, H_C, DH_C),
                             lambda b, p, pt: (pt[b, jnp.minimum(p, last)], layer, 0, 0, 0))
    per_b = lambda shape: pl.BlockSpec((1,) + shape, lambda b, p, pt: (b, 0, 0))
    grid_spec = pltpu.PrefetchScalarGridSpec(
        num_scalar_prefetch=1,
        grid=(nb, n_pages + 1),
        in_specs=[per_b((rows, DH_C)), page_spec, page_spec,
                  pl.BlockSpec((1, 1, 1, cols), lambda b, p, pt: (b, jnp.minimum(p, last), 0, 0)),
                  per_b((ncol, DH_C)), per_b((ncol, DH_C)), per_b((1, ncol)), per_b((rows, DH_C))],
        out_specs=per_b((rows, DH_C)),
        scratch_shapes=[pltpu.VMEM((rows, LANES), F32), pltpu.VMEM((rows, LANES), F32),
                        pltpu.VMEM((rows, DH_C), F32), pltpu.VMEM((rows, cols), F32)],
    )
    return pl.pallas_call(
        functools.partial(_fox_decode_body, n_pages, ps, n_new, new_pad),
        grid_spec=grid_spec,
        out_shape=jax.ShapeDtypeStruct((nb, rows, DH_C), BF16),
        compiler_params=_cparams(("parallel", "arbitrary")),
        name="fox_decode",
    )(page_table, q_rows, cache_k, cache_v, g_rows, k_new, v_new, hn_rows, z_rows)


def _merge_body(final, ya_ref, yb_ref, yc_ref, g0_ref, g1_ref, g2_ref, x_ref, gate_ref,
                wa_ref, wb_ref, wc_ref, wo_ref, gf_ref, xo_ref, *maybe_y):
    merged = (g0_ref[...] * _dot(ya_ref[...], wa_ref[...])
              + g1_ref[...] * _dot(yb_ref[...], wb_ref[...])
              + g2_ref[...] * _dot(yc_ref[...], wc_ref[...]))
    xn = x_ref[...] + gate_ref[...] * _dot(merged.astype(BF16), wo_ref[...])
    xo_ref[...] = xn
    if final:
        ms = jnp.mean(xn * xn, axis=-1, keepdims=True)
        maybe_y[0][...] = xn * lax.rsqrt(ms + NORM_EPS) * gf_ref[...]


def _merge(ya, yb, yc, g, x2, gate, wa, wb, wc, wo, g_final, tm, seq, final):
    m, d = x2.shape
    row = pl.BlockSpec((tm, d), lambda i: (i, 0))
    gspec = lambda k: pl.BlockSpec((tm, d), lambda i: (i, k))
    wspec = pl.BlockSpec((d, d), lambda i: (0, 0))
    out_shape = [jax.ShapeDtypeStruct((m, d), F32)]
    out_specs = [row]
    if final:
        out_shape.append(jax.ShapeDtypeStruct((m, d), F32))
        out_specs.append(row)
    return pl.pallas_call(
        functools.partial(_merge_body, final),
        grid=(m // tm,),
        in_specs=[row, row, row, gspec(0), gspec(1), gspec(2), row, _mod_spec(gate, tm, seq),
                  wspec, wspec, wspec, wspec, pl.BlockSpec((1, d), lambda i: (0, 0))],
        out_specs=out_specs,
        out_shape=out_shape,
        compiler_params=_cparams(("parallel",)),
        name="merge_out",
    )(ya, yb, yc, g, g, g, x2, gate, wa, wb, wc, wo, g_final.reshape(1, d))


def _col_offsets(d):
    sizes = (H_A * DK_A, H_A * DK_A, W_A, W_A,
             H_B * DK_B, H_B * DK_B, W_B, W_B,
             W_C, W_C, W_C, H_C, W_C,
             N_BRANCH * d)
    offs = np.concatenate([[0], np.cumsum(sizes)])
    return [(int(offs[i]), int(offs[i + 1])) for i in range(len(sizes))]


def _rope_tables(pos):
    half = DK_B // 2
    inv = ROPE_BASE ** (-jnp.arange(half, dtype=F32) / half)
    ang = pos.astype(F32)[:, None] * inv[None, :]
    cos = jnp.cos(ang)
    sin = jnp.sin(ang)
    return jnp.concatenate([cos, cos], axis=-1), jnp.concatenate([-sin, sin], axis=-1)


def _pad_tokens(a, batch, seq, seq_pad):
    n = a.shape[-1]
    return jnp.pad(a.reshape(batch, seq, n), ((0, 0), (0, seq_pad - seq), (0, 0))).reshape(batch * seq_pad, n)


def _unpad_tokens(a, batch, seq, seq_pad):
    n = a.shape[-1]
    return a.reshape(batch, seq_pad, n)[:, :seq].reshape(batch * seq, n)


def _layer(layer, x2, batch, seq, mod_l, pos, w, s_hgrn, s_ret, past, final):
    m, d = x2.shape
    tm = min(512, m)
    shift, scale, gate = _mod_arrays(mod_l, batch, seq, tm, d)
    u = _norm_mod(x2, w["g_norm"], shift, scale, tm, seq)

    cols = _col_offsets(d)
    w_in = w["w_in"]
    wcol = lambda i: w_in[:, cols[i][0]:cols[i][1]].astype(BF16)

    logf_a, k_a, q_a, v_a, z_a = _proj_hgrn(u, wcol(0), wcol(1), wcol(2), wcol(3), w["lb_logits"], layer, tm)
    cos, sin = _rope_tables(pos)
    if cos.shape[0] % tm != 0:
        cos = jnp.tile(cos, (m // cos.shape[0], 1))
        sin = jnp.tile(sin, (m // sin.shape[0], 1))
    q_b, k_b, v_b, z_b = _proj_ret(u, wcol(4), wcol(5), wcol(6), wcol(7), cos, sin, tm)
    q_c, k_c32, k_c16, v_c32, v_c16, z_c = _proj_fox(u, wcol(8), wcol(9), wcol(10), wcol(12), tm)
    logf_c = _proj_foxf(u, w_in[:, cols[11][0]:cols[11][1]], w["b_fox_f"], tm)
    gates = _proj_gate(u, wcol(13), tm)

    if past is None:
        y_a, s_hgrn_new = _hgrn_scan(logf_a, k_a, q_a, v_a, z_a, w["g_hgrn"], s_hgrn, batch, seq)
        y_b, s_ret_new = _ret_scan(q_b, k_b, v_b, z_b, w["g_ret"], s_ret, batch, seq,
                                   min(RET_CHUNK, seq), min(RET_CHUNK, seq))
        f_row = _fcum(logf_c.reshape(batch, seq, H_C).transpose(0, 2, 1)).reshape(batch, H_C, 1, seq)
        y_c = _fox_prompt(q_c, k_c16, v_c16, f_row, z_c, batch, seq)
    else:
        cache_k, cache_v, cache_logf, page_table = past
        sp = HGRN_CHUNK
        pad = lambda a: _pad_tokens(a, batch, seq, sp)
        y_a, s_hgrn_new = _hgrn_scan(pad(logf_a), pad(k_a), pad(q_a), pad(v_a), pad(z_a),
                                     w["g_hgrn"], s_hgrn, batch, sp)
        y_a = _unpad_tokens(y_a, batch, seq, sp)
        y_b, s_ret_new = _ret_scan(pad(q_b), pad(k_b), pad(v_b), pad(z_b), w["g_ret"], s_ret,
                                   batch, sp, sp, seq)
        y_b = _unpad_tokens(y_b, batch, seq, sp)

        ps = cache_k.shape[2]
        n_pages = page_table.shape[1]
        new_pad = LANES // H_C
        g_rows = _past_bias(cache_logf, page_table, layer).reshape(batch, n_pages, 1, ps * H_C)
        hn = _new_bias(_pad_tokens(logf_c, batch, seq, ps).reshape(batch, ps, H_C))
        hn_rows = hn[:, :new_pad].reshape(batch, 1, new_pad * H_C)
        to_rows = lambda a: a.reshape(batch, seq, H_C, DH_C).transpose(0, 2, 1, 3).reshape(batch, H_C * seq, DH_C)
        new_rows = lambda a: _pad_tokens(a, batch, seq, new_pad).reshape(batch, new_pad * H_C, DH_C)
        y_rows = _fox_decode(to_rows(q_c), cache_k, cache_v, g_rows, new_rows(k_c16), new_rows(v_c16),
                             hn_rows, to_rows(z_c), page_table, layer, seq, new_pad)
        y_c = y_rows.reshape(batch, H_C, seq, DH_C).transpose(0, 2, 1, 3).reshape(m, W_C)

    bf = lambda name: w[name].astype(BF16)
    outs = _merge(y_a, y_b, y_c, gates, x2, gate, bf("w_proj_a"), bf("w_proj_b"), bf("w_proj_c"),
                  bf("w_out"), w["g_final"], min(256, m), seq, final)
    return outs, s_hgrn_new, s_ret_new, k_c32, v_c32, logf_c


def kernel(x_prompt, x_sample, c_prompt, c_sample, cache_k, cache_v, cache_logf, state_hgrn, state_ret,
           page_table, w_ada, b_ada, g_norm, w_in, b_fox_f, lb_logits, g_hgrn, g_ret, w_proj_a, w_proj_b,
           w_proj_c, w_out, g_final):
    bp, lp, d = x_prompt.shape
    bs, ls, _ = x_sample.shape
    depth = w_in.shape[0]
    n_pages = page_table.shape[1]
    past_len = n_pages * cache_k.shape[2]
    pos_p = jnp.arange(lp, dtype=jnp.int32)
    pos_s = past_len + jnp.arange(ls, dtype=jnp.int32)

    n_c = bp + bs
    rows = -(-n_c // 8) * 8
    c_all = jnp.pad(jnp.concatenate([c_prompt, c_sample], axis=0), ((0, rows - n_c), (0, 0)))
    mod = _ada_mod(c_all, w_ada, b_ada)

    zero_hgrn = jnp.zeros((bp, H_A, DK_A, DV_A), F32)
    zero_ret = jnp.zeros((bp, H_B, DK_B, DV_B), F32)
    xp = x_prompt.reshape(bp * lp, d)
    xs = x_sample.reshape(bs * ls, d)
    acc = {k: [] for k in ("kp", "vp", "fp", "hp", "rp", "ks", "vs", "fs", "hs", "rs")}
    yp = ys = None
    for l in range(depth):
        w = dict(g_norm=g_norm[l], w_in=w_in[l], b_fox_f=b_fox_f[l], lb_logits=lb_logits,
                 g_hgrn=g_hgrn[l], g_ret=g_ret[l], w_proj_a=w_proj_a[l], w_proj_b=w_proj_b[l],
                 w_proj_c=w_proj_c[l], w_out=w_out[l], g_final=g_final)
        final = l == depth - 1
        outs, h, r, k, v, f = _layer(l, xp, bp, lp, mod[l, :bp], pos_p, w, zero_hgrn, zero_ret, None, final)
        xp = outs[0]
        if final:
            yp = outs[1]
        acc["kp"].append(k.reshape(bp, lp, H_C, DH_C)); acc["vp"].append(v.reshape(bp, lp, H_C, DH_C))
        acc["fp"].append(f.reshape(bp, lp, H_C)); acc["hp"].append(h); acc["rp"].append(r)

        outs, h, r, k, v, f = _layer(l, xs, bs, ls, mod[l, bp:bp + bs], pos_s, w, state_hgrn[:, l],
                                     state_ret[:, l], (cache_k, cache_v, cache_logf, page_table), final)
        xs = outs[0]
        if final:
            ys = outs[1]
        acc["ks"].append(k.reshape(bs, ls, H_C, DH_C)); acc["vs"].append(v.reshape(bs, ls, H_C, DH_C))
        acc["fs"].append(f.reshape(bs, ls, H_C)); acc["hs"].append(h); acc["rs"].append(r)

    st = lambda name: jnp.stack(acc[name], axis=1)
    return (yp.reshape(bp, lp, d), ys.reshape(bs, ls, d),
            st("kp"), st("vp"), st("fp"), st("hp"), st("rp"),
            st("ks"), st("vs"), st("fs"), st("hs"), st("rs"))
```

```python
import functools
import math

import numpy as np
import jax
import jax.numpy as jnp
from jax import lax
from jax.experimental import pallas as pl
from jax.experimental.pallas import tpu as pltpu

F32 = jnp.float32
BF16 = jnp.bfloat16

H_A, DK_A, DV_A = 8, 128, 128
H_B, DK_B, DV_B = 4, 128, 256
H_C, DH_C = 8, 128
W_A = H_A * DV_A
W_B = H_B * DV_B
W_C = H_C * DH_C
N_BRANCH = 3
ROPE_BASE = 10000.0
RET_DECAY_BASE = 5.0
NORM_EPS = 1e-6
MASK_VALUE = -1e30
LOG2E = math.log2(math.e)

LANES = 128
HGRN_CHUNK = 128
HGRN_HEADS_PER_STEP = 4
DECODE_PAGES_PER_STEP = 8
RET_CHUNK = 256
FOX_BLOCK = 512
VMEM_LIMIT = 56 * 1024 * 1024


def _cparams(sem):
    return pltpu.CompilerParams(dimension_semantics=sem, vmem_limit_bytes=VMEM_LIMIT)


def _dot(a, b):
    return jnp.dot(a, b, preferred_element_type=F32)


def _dot_nt(a, b):
    return lax.dot_general(a, b, (((1,), (1,)), ((), ())), preferred_element_type=F32)


def _split3(x):
    hi = x.astype(BF16)
    r = x - hi.astype(F32)
    mid = r.astype(BF16)
    lo = (r - mid.astype(F32)).astype(BF16)
    return hi, mid, lo


def _dot_exact_lhs(t, x, terms=3):
    parts = _split3(x)[:terms]
    out = _dot(t, parts[0])
    for part in parts[1:]:
        out = out + _dot(t, part)
    return out


def _dot_exact_rhs(x, t):
    hi, mid, lo = _split3(x)
    return _dot(hi, t) + _dot(mid, t) + _dot(lo, t)


def _sigmoid(x):
    return 1.0 / (1.0 + jnp.exp(-x))


def _silu(x):
    return x * _sigmoid(x)


def _log_sigmoid(x):
    return jnp.minimum(x, 0.0) - jnp.log(1.0 + jnp.exp(-jnp.abs(x)))


def _ada_body(c_ref, w_ref, b_ref, o_ref):
    c = c_ref[...]
    w = w_ref[0]
    c_hi = c.astype(BF16)
    c_lo = (c - c_hi.astype(F32)).astype(BF16)
    w_hi = w.astype(BF16)
    w_lo = (w - w_hi.astype(F32)).astype(BF16)
    o_ref[0] = _dot(c_hi, w_hi) + _dot(c_hi, w_lo) + _dot(c_lo, w_hi) + b_ref[0]


def _ada_mod(c_all, w_ada, b_ada):
    depth, d, d3 = w_ada.shape
    rows = c_all.shape[0]
    return pl.pallas_call(
        _ada_body,
        grid=(depth, d3 // d),
        in_specs=[pl.BlockSpec((rows, d), lambda l, j: (0, 0)),
                  pl.BlockSpec((1, d, d), lambda l, j: (l, 0, j)),
                  pl.BlockSpec((1, 1, d), lambda l, j: (l, 0, j))],
        out_specs=pl.BlockSpec((1, rows, d), lambda l, j: (l, 0, j)),
        out_shape=jax.ShapeDtypeStruct((depth, rows, d3), F32),
        compiler_params=_cparams(("arbitrary", "arbitrary")),
        name="ada_mod",
    )(c_all, w_ada, b_ada.reshape(depth, 1, d3))


def _mod_spec(mod, tm, seq):
    if mod.ndim == 3:
        return pl.BlockSpec((None, 1, mod.shape[-1]), lambda i: ((i * tm) // seq, 0, 0))
    return pl.BlockSpec((tm, mod.shape[-1]), lambda i: (i, 0))


def _mod_arrays(mod_l, batch, seq, tm, d):
    parts = [mod_l[:, k * d:(k + 1) * d] for k in range(3)]
    if seq % tm == 0:
        return [p.reshape(batch, 1, d) for p in parts]
    return [jnp.broadcast_to(p[:, None, :], (batch, seq, d)).reshape(batch * seq, d) for p in parts]


def _norm_body(x_ref, g_ref, sh_ref, sc_ref, u_ref):
    x = x_ref[...]
    ms = jnp.mean(x * x, axis=-1, keepdims=True)
    y = x * lax.rsqrt(ms + NORM_EPS) * g_ref[...]
    u_ref[...] = (y * (1.0 + sc_ref[...]) + sh_ref[...]).astype(BF16)


def _norm_mod(x2, g, shift, scale, tm, seq):
    m, d = x2.shape
    return pl.pallas_call(
        _norm_body,
        grid=(m // tm,),
        in_specs=[pl.BlockSpec((tm, d), lambda i: (i, 0)),
                  pl.BlockSpec((1, d), lambda i: (0, 0)),
                  _mod_spec(shift, tm, seq), _mod_spec(scale, tm, seq)],
        out_specs=pl.BlockSpec((tm, d), lambda i: (i, 0)),
        out_shape=jax.ShapeDtypeStruct((m, d), BF16),
        compiler_params=_cparams(("parallel",)),
        name="norm_mod",
    )(x2, g.reshape(1, d), shift, scale)


def _proj_hgrn_body(layer, u_ref, wf_ref, wq_ref, wi_ref, wz_ref, lbl_ref,
                    logf_ref, k_ref, q_ref, v_ref, z_ref):
    u = u_ref[...]
    lbl = lbl_ref[...]
    e = jnp.exp(lbl - jnp.max(lbl, axis=0, keepdims=True))
    sm = e / jnp.sum(e, axis=0, keepdims=True)
    lb = jnp.zeros_like(sm[0:1])
    for r in range(1, layer + 1):
        lb = lb + sm[r:r + 1]
    fl = _dot(u, wf_ref[...])
    logf_ref[...] = jnp.log(lb + (1.0 - lb) * _sigmoid(fl))
    k_ref[...] = (1.0 - lb) * _sigmoid(-fl)
    q_ref[...] = _silu(_dot(u, wq_ref[...])) * (DK_A ** -0.5)
    v_ref[...] = _dot(u, wi_ref[...]).astype(BF16)
    z_ref[...] = _silu(_dot(u, wz_ref[...]))


def _proj_hgrn(u, wf, wq, wi, wz, lb_logits, layer, tm, tn=256):
    m, d = u.shape
    n = wf.shape[1]
    depth = lb_logits.shape[0]
    wspec = pl.BlockSpec((d, tn), lambda j, i: (0, j))
    ospec = pl.BlockSpec((tm, tn), lambda j, i: (i, j))
    return pl.pallas_call(
        functools.partial(_proj_hgrn_body, layer),
        grid=(n // tn, m // tm),
        in_specs=[pl.BlockSpec((tm, d), lambda j, i: (i, 0)), wspec, wspec, wspec, wspec,
                  pl.BlockSpec((depth, tn), lambda j, i: (0, j))],
        out_specs=[ospec] * 5,
        out_shape=[jax.ShapeDtypeStruct((m, n), F32), jax.ShapeDtypeStruct((m, n), F32),
                   jax.ShapeDtypeStruct((m, n), F32), jax.ShapeDtypeStruct((m, n), BF16),
                   jax.ShapeDtypeStruct((m, n), F32)],
        compiler_params=_cparams(("parallel", "parallel")),
        name="proj_hgrn",
    )(u, wf, wq, wi, wz, lb_logits)


def _rope(x, cos, sin_signed):
    return x * cos + pltpu.roll(x, DK_B // 2, axis=1) * sin_signed


def _proj_ret_body(u_ref, wq_ref, wk_ref, wv_ref, wz_ref, cos_ref, sin_ref,
                   q_ref, k_ref, v_ref, z_ref):
    u = u_ref[...]
    cos = cos_ref[...]
    sin = sin_ref[...]
    q_ref[...] = _rope(_dot(u, wq_ref[...]), cos, sin) * (DK_B ** -0.5)
    k_ref[...] = _rope(_dot(u, wk_ref[...]), cos, sin)
    v_ref[...] = _dot(u, wv_ref[...]).astype(BF16)
    z_ref[...] = _silu(_dot(u, wz_ref[...]))


def _proj_ret(u, wq, wk, wv, wz, cos, sin, tm):
    m, d = u.shape
    n_rope_tiles = cos.shape[0] // tm
    qk_w = pl.BlockSpec((d, DK_B), lambda h, i: (0, h))
    vz_w = pl.BlockSpec((d, DV_B), lambda h, i: (0, h))
    rope_spec = pl.BlockSpec((tm, DK_B), lambda h, i: (i % n_rope_tiles, 0))
    qk_o = pl.BlockSpec((tm, DK_B), lambda h, i: (i, h))
    vz_o = pl.BlockSpec((tm, DV_B), lambda h, i: (i, h))
    return pl.pallas_call(
        _proj_ret_body,
        grid=(H_B, m // tm),
        in_specs=[pl.BlockSpec((tm, d), lambda h, i: (i, 0)), qk_w, qk_w, vz_w, vz_w,
                  rope_spec, rope_spec],
        out_specs=[qk_o, qk_o, vz_o, vz_o],
        out_shape=[jax.ShapeDtypeStruct((m, H_B * DK_B), F32), jax.ShapeDtypeStruct((m, H_B * DK_B), F32),
                   jax.ShapeDtypeStruct((m, W_B), BF16), jax.ShapeDtypeStruct((m, W_B), F32)],
        compiler_params=_cparams(("parallel", "parallel")),
        name="proj_ret",
    )(u, wq, wk, wv, wz, cos, sin)


def _proj_fox_body(u_ref, wq_ref, wk_ref, wv_ref, wz_ref,
                   q16_ref, k32_ref, k16_ref, v32_ref, v16_ref, z_ref):
    u = u_ref[...]
    q16_ref[...] = _dot(u, wq_ref[...]).astype(BF16)
    k = _dot(u, wk_ref[...])
    k32_ref[...] = k
    k16_ref[...] = k.astype(BF16)
    v = _dot(u, wv_ref[...])
    v32_ref[...] = v
    v16_ref[...] = v.astype(BF16)
    z_ref[...] = _silu(_dot(u, wz_ref[...]))


def _proj_fox(u, wq, wk, wv, wz, tm, tn=256):
    m, d = u.shape
    n = wq.shape[1]
    wspec = pl.BlockSpec((d, tn), lambda j, i: (0, j))
    ospec = pl.BlockSpec((tm, tn), lambda j, i: (i, j))
    shp = lambda dt: jax.ShapeDtypeStruct((m, n), dt)
    return pl.pallas_call(
        _proj_fox_body,
        grid=(n // tn, m // tm),
        in_specs=[pl.BlockSpec((tm, d), lambda j, i: (i, 0)), wspec, wspec, wspec, wspec],
        out_specs=[ospec] * 6,
        out_shape=[shp(BF16), shp(F32), shp(BF16), shp(F32), shp(BF16), shp(F32)],
        compiler_params=_cparams(("parallel", "parallel")),
        name="proj_fox",
    )(u, wq, wk, wv, wz)


def _proj_foxf_body(u_ref, whi_ref, wlo_ref, b_ref, o_ref):
    u = u_ref[...]
    cf = _dot(u, whi_ref[...]) + _dot(u, wlo_ref[...]) + b_ref[...]
    o_ref[...] = _log_sigmoid(cf)[:, :H_C]


def _proj_foxf(u, wf, bf, tm):
    m, d = u.shape
    wpad = jnp.pad(wf, ((0, 0), (0, LANES - H_C)))
    whi = wpad.astype(BF16)
    wlo = (wpad - whi.astype(F32)).astype(BF16)
    bpad = jnp.pad(bf.reshape(1, H_C), ((0, 0), (0, LANES - H_C)))
    wspec = pl.BlockSpec((d, LANES), lambda i: (0, 0))
    return pl.pallas_call(
        _proj_foxf_body,
        grid=(m // tm,),
        in_specs=[pl.BlockSpec((tm, d), lambda i: (i, 0)), wspec, wspec,
                  pl.BlockSpec((1, LANES), lambda i: (0, 0))],
        out_specs=pl.BlockSpec((tm, H_C), lambda i: (i, 0)),
        out_shape=jax.ShapeDtypeStruct((m, H_C), F32),
        compiler_params=_cparams(("parallel",)),
        name="proj_foxf",
    )(u, whi, wlo, bpad)


def _proj_gate_body(u_ref, w_ref, o_ref):
    o_ref[...] = _sigmoid(_dot(u_ref[...], w_ref[...]))


def _proj_gate(u, wg, tm, tn=512):
    m, d = u.shape
    n = wg.shape[1]
    return pl.pallas_call(
        _proj_gate_body,
        grid=(n // tn, m // tm),
        in_specs=[pl.BlockSpec((tm, d), lambda j, i: (i, 0)),
                  pl.BlockSpec((d, tn), lambda j, i: (0, j))],
        out_specs=pl.BlockSpec((tm, tn), lambda j, i: (i, j)),
        out_shape=jax.ShapeDtypeStruct((m, n), F32),
        compiler_params=_cparams(("parallel", "parallel")),
        name="proj_gate",
    )(u, wg)


def _hgrn_levels(c):
    return [c >> (i + 1) for i in range(int(math.log2(c)))]


def _hgrn_tables(c):
    t = np.arange(c)[:, None]
    u = np.arange(c)[None, :]
    mats, masks = [], []
    for w in _hgrn_levels(c):
        r = (t // (2 * w)) * (2 * w) + w
        upper = (t % (2 * w)) >= w
        mats.append(np.where(upper, (u >= r) & (u <= t), (u > t) & (u < r)))
        masks.append((t // (2 * w)) == (u // (2 * w)))
    mats.append(u <= t)
    mats.append(u > t)
    masks.append(t == u)
    return (jnp.asarray(np.concatenate(mats, 0).astype(np.float32), BF16),
            jnp.asarray(np.stack(masks, 0).astype(np.float32), F32))


def _hgrn_body(c, nc, hp, logf_ref, k_ref, q_ref, v_ref, z_ref, g_ref, s0_ref, t_ref, mask_ref,
               y_ref, sout_ref, st_ref):
    ci = pl.program_id(2)

    @pl.when(ci == 0)
    def _():
        for hh in range(hp):
            st_ref[hh] = s0_ref[0, hh].T

    levels = _hgrn_levels(c)
    nl = len(levels)
    e_all = jnp.exp(_dot_exact_lhs(t_ref[...], logf_ref[...], terms=2))
    row = lax.broadcasted_iota(jnp.int32, (c, DK_A), 0)
    g = g_ref[...]
    for hh in range(hp):
        cols = slice(hh * DK_A, (hh + 1) * DK_A)
        q = q_ref[:, cols]
        k = k_ref[:, cols]
        v = v_ref[:, cols]
        a = mask_ref[nl] * _dot_nt(q.astype(BF16), k.astype(BF16))
        for li, w in enumerate(levels):
            e_w = e_all[li * c:(li + 1) * c, cols]
            upper = (row & w) != 0
            q_w = jnp.where(upper, q * e_w, 0.0).astype(BF16)
            k_w = jnp.where(upper, 0.0, k * e_w).astype(BF16)
            a = a + mask_ref[li] * _dot_nt(q_w, k_w)
        e_b = e_all[nl * c:(nl + 1) * c, cols]
        e_end = e_all[(nl + 1) * c:(nl + 2) * c, cols]
        st = st_ref[hh]
        o = _dot(a.astype(BF16), v) + _dot_nt((q * e_b).astype(BF16), st.astype(BF16))
        v_t = v.astype(F32).T.astype(BF16)
        st_ref[hh] = st * e_b[c - 1:c, :] + _dot(v_t, (k * e_end).astype(BF16))
        ms = jnp.mean(o * o, axis=-1, keepdims=True)
        y_ref[:, cols] = (o * lax.rsqrt(ms + NORM_EPS) * g * z_ref[:, cols]).astype(BF16)

    @pl.when(ci == nc - 1)
    def _():
        for hh in range(hp):
            sout_ref[0, hh] = st_ref[hh].T


def _hgrn_scan(logf, k, q, v, z, g, s0, batch, seq):
    c = HGRN_CHUNK
    hp = HGRN_HEADS_PER_STEP
    nc = seq // c
    m = batch * seq
    tmat, masks = _hgrn_tables(c)
    tok = pl.BlockSpec((c, hp * DK_A), lambda b, h, ci: (b * nc + ci, h))
    st_spec = pl.BlockSpec((1, hp, DK_A, DV_A), lambda b, h, ci: (b, h, 0, 0))
    return pl.pallas_call(
        functools.partial(_hgrn_body, c, nc, hp),
        grid=(batch, H_A // hp, nc),
        in_specs=[tok, tok, tok, tok, tok,
                  pl.BlockSpec((1, DV_A), lambda b, h, ci: (0, 0)),
                  st_spec,
                  pl.BlockSpec(tmat.shape, lambda b, h, ci: (0, 0)),
                  pl.BlockSpec(masks.shape, lambda b, h, ci: (0, 0, 0))],
        out_specs=[tok, st_spec],
        out_shape=[jax.ShapeDtypeStruct((m, W_A), BF16),
                   jax.ShapeDtypeStruct((batch, H_A, DK_A, DV_A), F32)],
        scratch_shapes=[pltpu.VMEM((hp, DV_A, DK_A), F32)],
        compiler_params=_cparams(("parallel", "parallel", "arbitrary")),
        name="hgrn_scan",
    )(logf, k, q, v, z, g.reshape(1, DV_A), s0, tmat, masks)


def _ret_body(c, c_eff, nc, lg_ref, q_ref, k_ref, v_ref, z_ref, g_ref, s0_ref,
              y_ref, sout_ref, s_ref):
    h = pl.program_id(1)
    ci = pl.program_id(2)

    @pl.when(ci == 0)
    def _():
        s_ref[...] = s0_ref[0, 0]

    lg = lg_ref[h]
    q = q_ref[...]
    k = k_ref[...]
    v = v_ref[...]
    t = lax.broadcasted_iota(jnp.int32, (c, DK_B), 0).astype(F32)
    q_dec = jnp.exp((t + 1.0) * lg)
    k_dec = jnp.where(t < c_eff, jnp.exp(jnp.maximum(c_eff - 1.0 - t, 0.0) * lg), 0.0)
    rel = (lax.broadcasted_iota(jnp.int32, (c, c), 0)
           - lax.broadcasted_iota(jnp.int32, (c, c), 1)).astype(F32)
    intra = jnp.where(rel >= 0.0, jnp.exp(jnp.maximum(rel, 0.0) * lg), 0.0)
    a = _dot_nt(q.astype(BF16), k.astype(BF16)) * intra
    s = s_ref[...]
    o = _dot(a.astype(BF16), v) + _dot((q * q_dec).astype(BF16), s.astype(BF16))
    kd_t = (k * k_dec).T.astype(BF16)
    chunk_dec = jnp.exp(jnp.full((1, DV_B), c_eff, F32) * lg)
    s_ref[...] = chunk_dec * s + _dot(kd_t, v)

    ms = jnp.mean(o * o, axis=-1, keepdims=True)
    y_ref[...] = (o * lax.rsqrt(ms + NORM_EPS) * g_ref[...] * z_ref[...]).astype(BF16)

    @pl.when(ci == nc - 1)
    def _():
        sout_ref[0, 0] = s_ref[...]


def _ret_scan(q, k, v, z, g, s0, batch, seq, c, c_eff):
    nc = seq // c
    m = batch * seq
    log_gamma = jnp.log1p(-jnp.power(2.0, -RET_DECAY_BASE - jnp.arange(H_B, dtype=F32)))
    qk = pl.BlockSpec((c, DK_B), lambda b, h, ci: (b * nc + ci, h))
    vz = pl.BlockSpec((c, DV_B), lambda b, h, ci: (b * nc + ci, h))
    st_spec = pl.BlockSpec((1, 1, DK_B, DV_B), lambda b, h, ci: (b, h, 0, 0))
    return pl.pallas_call(
        functools.partial(_ret_body, c, c_eff, nc),
        grid=(batch, H_B, nc),
        in_specs=[pl.BlockSpec(memory_space=pltpu.SMEM), qk, qk, vz, vz,
                  pl.BlockSpec((1, DV_B), lambda b, h, ci: (0, 0)), st_spec],
        out_specs=[vz, st_spec],
        out_shape=[jax.ShapeDtypeStruct((m, W_B), BF16),
                   jax.ShapeDtypeStruct((batch, H_B, DK_B, DV_B), F32)],
        scratch_shapes=[pltpu.VMEM((DK_B, DV_B), F32)],
        compiler_params=_cparams(("parallel", "parallel", "arbitrary")),
        name="ret_scan",
    )(log_gamma, q, k, v, z, g.reshape(1, DV_B), s0)


def _fcum_body(nblk, x_ref, o_ref):
    r = lax.broadcasted_iota(jnp.int32, (LANES, LANES), 0)
    cc = lax.broadcasted_iota(jnp.int32, (LANES, LANES), 1)
    tri = jnp.where(r <= cc, 1.0, 0.0).astype(BF16)
    carry = jnp.zeros((H_C, 1), F32)
    for blk in range(nblk):
        x = x_ref[0, :, blk * LANES:(blk + 1) * LANES]
        y = _dot_exact_rhs(x, tri) + carry
        o_ref[0, :, blk * LANES:(blk + 1) * LANES] = y * LOG2E
        carry = y[:, LANES - 1:LANES]


def _fcum(logf_t):
    b, h, seq = logf_t.shape
    spec = pl.BlockSpec((1, h, seq), lambda i: (i, 0, 0))
    return pl.pallas_call(
        functools.partial(_fcum_body, seq // LANES),
        grid=(b,),
        in_specs=[spec], out_specs=spec,
        out_shape=jax.ShapeDtypeStruct((b, h, seq), F32),
        compiler_params=_cparams(("parallel",)),
        name="fox_fcum",
    )(logf_t)


def _fold_lanes(x, op):
    out = x[:, 0:LANES]
    for c in range(1, x.shape[1] // LANES):
        out = op(out, x[:, c * LANES:(c + 1) * LANES])
    return out


def _fox_prompt_body(tq, q_ref, k_ref, v_ref, fq_ref, fk_ref, z_ref, o_ref, fqc_ref, t_ref, p_ref):
    i = pl.program_id(2)
    q = (q_ref[...].astype(F32) * (DH_C ** -0.5 * LOG2E)).astype(BF16)
    fqc_ref[...] = jnp.broadcast_to(fq_ref[0, 0], (LANES, tq)).T
    fq = fqc_ref[:, 0:1]

    def scores(j):
        start = pl.multiple_of(j * tq, tq)
        return _dot_nt(q, k_ref[pl.ds(start, tq), :]) - fk_ref[0, 0, :, pl.ds(start, tq)]

    def softmax(t, m_prev, l_prev):
        m_new = jnp.maximum(m_prev, jnp.max(_fold_lanes(t, jnp.maximum), axis=-1, keepdims=True) + fq)
        p = jnp.exp2(t + (fq - m_new))
        alpha = jnp.exp2(m_prev - m_new)
        l_new = alpha * l_prev + jnp.sum(_fold_lanes(p, jnp.add), axis=-1, keepdims=True)
        return p.astype(BF16), alpha, m_new, l_new

    def weighted(j, p, alpha, acc):
        start = pl.multiple_of(j * tq, tq)
        return alpha * acc + _dot(p, v_ref[pl.ds(start, tq), :])

    t_ref[0] = scores(0)
    p_ref[1] = jnp.zeros((tq, tq), BF16)

    def trip(j, carry):
        alpha_prev, m_prev, l_prev, acc = carry
        slot = j % 2
        acc = weighted(jnp.maximum(j - 1, 0), p_ref[1 - slot], alpha_prev, acc)
        p, alpha, m_new, l_new = softmax(t_ref[slot], m_prev, l_prev)
        p_ref[slot] = p
        t_ref[1 - slot] = scores(j + 1)
        return alpha, m_new, l_new, acc

    init = (jnp.ones((tq, 1), F32), jnp.full((tq, 1), MASK_VALUE, F32), jnp.zeros((tq, 1), F32),
            jnp.zeros((tq, DH_C), F32))
    alpha_prev, m_prev, l_prev, acc = lax.fori_loop(0, i, trip, init)
    slot = i % 2
    acc = weighted(jnp.maximum(i - 1, 0), p_ref[1 - slot], alpha_prev, acc)
    rr = lax.broadcasted_iota(jnp.int32, (tq, tq), 0)
    cc = lax.broadcasted_iota(jnp.int32, (tq, tq), 1)
    p, alpha, _, l_fin = softmax(jnp.where(rr >= cc, t_ref[slot], MASK_VALUE), m_prev, l_prev)
    acc = weighted(i, p, alpha, acc)
    o_ref[...] = (acc / l_fin * z_ref[...]).astype(BF16)


def _fox_prompt(q16, k16, v16, f_row, z, batch, seq):
    tq = min(FOX_BLOCK, seq)
    nq = seq // tq
    m = batch * seq
    qspec = pl.BlockSpec((tq, DH_C), lambda b, h, i: (b * nq + i, h))
    kvspec = pl.BlockSpec((seq, DH_C), lambda b, h, i: (b, h))
    return pl.pallas_call(
        functools.partial(_fox_prompt_body, tq),
        grid=(batch, H_C, nq),
        in_specs=[qspec, kvspec, kvspec,
                  pl.BlockSpec((1, 1, 1, tq), lambda b, h, i: (b, h, 0, i)),
                  pl.BlockSpec((1, 1, 1, seq), lambda b, h, i: (b, h, 0, 0)),
                  qspec],
        out_specs=qspec,
        out_shape=jax.ShapeDtypeStruct((m, W_C), BF16),
        scratch_shapes=[pltpu.VMEM((tq, LANES), F32), pltpu.VMEM((2, tq, tq), F32),
                        pltpu.VMEM((2, tq, tq), BF16)],
        compiler_params=_cparams(("parallel", "parallel", "arbitrary")),
        name="fox_prompt",
    )(q16, k16, v16, f_row, f_row, z)


def _tri_f32(n, kind):
    r = lax.broadcasted_iota(jnp.int32, (n, n), 0)
    c = lax.broadcasted_iota(jnp.int32, (n, n), 1)
    cond = (c > r) if kind == "after" else (c <= r)
    return jnp.where(cond, 1.0, 0.0).astype(BF16)


def _new_bias_body(n, x_ref, o_ref):
    o_ref[0] = -_dot_exact_lhs(_tri_f32(n, "upto"), x_ref[0])


def _new_bias(logf_new_pad):
    nb, n, h = logf_new_pad.shape
    spec = pl.BlockSpec((1, n, h), lambda b: (b, 0, 0))
    return pl.pallas_call(
        functools.partial(_new_bias_body, n),
        grid=(nb,),
        in_specs=[spec], out_specs=spec,
        out_shape=jax.ShapeDtypeStruct((nb, n, h), F32),
        compiler_params=_cparams(("parallel",)),
        name="fox_new_bias",
    )(logf_new_pad)


def _fox_decode_body(n_steps, pg, ps, n_new, new_pad, pt_ref, q_ref, *refs):
    k_refs = refs[:pg]
    v_refs = refs[pg:2 * pg]
    lf_refs = refs[2 * pg:3 * pg]
    rep_ref, kn_ref, vn_ref, hn_ref, z_ref, o_ref, m_ref, l_ref, acc_ref, carry_ref = refs[3 * pg:]
    step = pl.program_id(1)
    scale = DH_C ** -0.5
    rows = H_C * n_new
    cols = ps * H_C

    @pl.when(step == 0)
    def _():
        m_ref[...] = jnp.full_like(m_ref, MASK_VALUE)
        l_ref[...] = jnp.zeros_like(l_ref)
        acc_ref[...] = jnp.zeros_like(acc_ref)
        carry_ref[...] = jnp.zeros_like(carry_ref)

    @pl.when(step < n_steps)
    def _():
        lf = jnp.concatenate([r[0, 0] for r in lf_refs], axis=0)
        t_from = lax.broadcasted_iota(jnp.int32, (ps, ps), 0)
        t_to = lax.broadcasted_iota(jnp.int32, (ps, ps), 1)
        later = jnp.where(t_from > t_to, 1.0, 0.0).astype(BF16)
        within = _dot_exact_rhs(lf, later)
        total = jnp.sum(lf, axis=-1, keepdims=True)
        carry = carry_ref[:, 0:1]
        bias = []
        for i in range(pg):
            bias.append(within[i * H_C:(i + 1) * H_C] + carry)
            carry = carry + total[i * H_C:(i + 1) * H_C]
        carry_ref[...] = jnp.broadcast_to(carry, carry_ref.shape)
        spread = _dot_exact_rhs(jnp.concatenate(bias, axis=0), rep_ref[...])
        other_head = jnp.where(lax.broadcasted_iota(jnp.int32, (H_C, cols), 0)
                               == lax.broadcasted_iota(jnp.int32, (H_C, cols), 1) % H_C, 0.0, MASK_VALUE)

        q = q_ref[0]
        s_list = []
        for i in range(pg):
            kf = k_refs[i][0, 0].reshape(cols, DH_C).astype(BF16)
            b8 = spread[i * H_C:(i + 1) * H_C] + other_head
            s_list.append(_dot_nt(q, kf) * scale + jnp.concatenate([b8] * n_new, axis=0))
        mx = s_list[0]
        for s in s_list[1:]:
            mx = jnp.maximum(mx, s)
        m_prev = m_ref[:, 0:1]
        m_new = jnp.maximum(m_prev, jnp.max(_fold_lanes(mx, jnp.maximum), axis=-1, keepdims=True))
        alpha = jnp.exp(m_prev - m_new)
        psum = None
        pv = None
        for i in range(pg):
            p = jnp.exp(s_list[i] - m_new)
            contrib = _dot(p.astype(BF16), v_refs[i][0, 0].reshape(cols, DH_C).astype(BF16))
            psum = p if psum is None else psum + p
            pv = contrib if pv is None else pv + contrib
        l_ref[...] = alpha * l_ref[...] + jnp.sum(_fold_lanes(psum, jnp.add), axis=-1, keepdims=True)
        acc_ref[...] = alpha * acc_ref[...] + pv
        m_ref[...] = jnp.broadcast_to(m_new, m_ref.shape)

    @pl.when(step == n_steps)
    def _():
        ncol = new_pad * H_C
        s = _dot_nt(q_ref[0], kn_ref[0]) * scale + hn_ref[0]
        rr = lax.broadcasted_iota(jnp.int32, (rows, ncol), 0)
        cc = lax.broadcasted_iota(jnp.int32, (rows, ncol), 1)
        tok = cc // H_C
        ok = ((rr % H_C) == (cc % H_C)) & (tok <= (rr // H_C)) & (tok < n_new)
        s = jnp.where(ok, s, MASK_VALUE)
        m_prev = m_ref[:, 0:1]
        m_new = jnp.maximum(m_prev, jnp.max(s, axis=-1, keepdims=True))
        p = jnp.exp(s - m_new)
        alpha = jnp.exp(m_prev - m_new)
        l_fin = alpha * l_ref[:, 0:1] + jnp.sum(p, axis=-1, keepdims=True)
        acc = alpha * acc_ref[...] + _dot(p.astype(BF16), vn_ref[0])
        o_ref[0] = (acc / l_fin * z_ref[0]).astype(BF16)


def _fox_decode(q_rows, cache_k, cache_v, lf_t, k_new, v_new, hn_rows, z_rows, page_table, layer,
                n_new, new_pad):
    nb, n_pages = page_table.shape
    ps = cache_k.shape[2]
    pg = min(DECODE_PAGES_PER_STEP, n_pages)
    n_steps = n_pages // pg
    rows = H_C * n_new
    cols = ps * H_C
    ncol = new_pad * H_C
    rep = jnp.asarray((np.arange(ps)[:, None] == np.arange(cols)[None, :] // H_C).astype(np.float32), BF16)

    def page_of(b, s, pt, i):
        return pt[b, n_pages - 1 - (jnp.minimum(s, n_steps - 1) * pg + i)]

    def kv_spec(i):
        return pl.BlockSpec((1, 1, ps, H_C, DH_C), lambda b, s, pt: (page_of(b, s, pt, i), layer, 0, 0, 0))

    def lf_spec(i):
        return pl.BlockSpec((1, 1, H_C, ps), lambda b, s, pt: (layer, page_of(b, s, pt, i), 0, 0))

    per_b = lambda shape: pl.BlockSpec((1,) + shape, lambda b, s, pt: (b, 0, 0))
    grid_spec = pltpu.PrefetchScalarGridSpec(
        num_scalar_prefetch=1,
        grid=(nb, n_steps + 1),
        in_specs=([per_b((rows, DH_C))] + [kv_spec(i) for i in range(pg)] + [kv_spec(i) for i in range(pg)]
                  + [lf_spec(i) for i in range(pg)]
                  + [pl.BlockSpec((ps, cols), lambda b, s, pt: (0, 0))]
                  + [per_b((ncol, DH_C)), per_b((ncol, DH_C)), per_b((1, ncol)), per_b((rows, DH_C))]),
        out_specs=per_b((rows, DH_C)),
        scratch_shapes=[pltpu.VMEM((rows, LANES), F32), pltpu.VMEM((rows, LANES), F32),
                        pltpu.VMEM((rows, DH_C), F32), pltpu.VMEM((H_C, LANES), F32)],
    )
    return pl.pallas_call(
        functools.partial(_fox_decode_body, n_steps, pg, ps, n_new, new_pad),
        grid_spec=grid_spec,
        out_shape=jax.ShapeDtypeStruct((nb, rows, DH_C), BF16),
        compiler_params=_cparams(("parallel", "arbitrary")),
        name="fox_decode",
    )(page_table, q_rows, *([cache_k] * pg), *([cache_v] * pg), *([lf_t] * pg), rep,
      k_new, v_new, hn_rows, z_rows)


def _merge_body(final, ya_ref, yb_ref, yc_ref, g0_ref, g1_ref, g2_ref, x_ref, gate_ref,
                wa_ref, wb_ref, wc_ref, wo_ref, gf_ref, xo_ref, *maybe_y):
    merged = (g0_ref[...] * _dot(ya_ref[...], wa_ref[...])
              + g1_ref[...] * _dot(yb_ref[...], wb_ref[...])
              + g2_ref[...] * _dot(yc_ref[...], wc_ref[...]))
    xn = x_ref[...] + gate_ref[...] * _dot(merged.astype(BF16), wo_ref[...])
    xo_ref[...] = xn
    if final:
        ms = jnp.mean(xn * xn, axis=-1, keepdims=True)
        maybe_y[0][...] = xn * lax.rsqrt(ms + NORM_EPS) * gf_ref[...]


def _merge(ya, yb, yc, g, x2, gate, wa, wb, wc, wo, g_final, tm, seq, final):
    m, d = x2.shape
    row = pl.BlockSpec((tm, d), lambda i: (i, 0))
    gspec = lambda k: pl.BlockSpec((tm, d), lambda i: (i, k))
    wspec = pl.BlockSpec((d, d), lambda i: (0, 0))
    out_shape = [jax.ShapeDtypeStruct((m, d), F32)]
    out_specs = [row]
    if final:
        out_shape.append(jax.ShapeDtypeStruct((m, d), F32))
        out_specs.append(row)
    return pl.pallas_call(
        functools.partial(_merge_body, final),
        grid=(m // tm,),
        in_specs=[row, row, row, gspec(0), gspec(1), gspec(2), row, _mod_spec(gate, tm, seq),
                  wspec, wspec, wspec, wspec, pl.BlockSpec((1, d), lambda i: (0, 0))],
        out_specs=out_specs,
        out_shape=out_shape,
        compiler_params=_cparams(("parallel",)),
        name="merge_out",
    )(ya, yb, yc, g, g, g, x2, gate, wa, wb, wc, wo, g_final.reshape(1, d))


def _col_offsets(d):
    sizes = (H_A * DK_A, H_A * DK_A, W_A, W_A,
             H_B * DK_B, H_B * DK_B, W_B, W_B,
             W_C, W_C, W_C, H_C, W_C,
             N_BRANCH * d)
    offs = np.concatenate([[0], np.cumsum(sizes)])
    return [(int(offs[i]), int(offs[i + 1])) for i in range(len(sizes))]


def _rope_tables(pos):
    half = DK_B // 2
    inv = ROPE_BASE ** (-jnp.arange(half, dtype=F32) / half)
    ang = pos.astype(F32)[:, None] * inv[None, :]
    cos = jnp.cos(ang)
    sin = jnp.sin(ang)
    return jnp.concatenate([cos, cos], axis=-1), jnp.concatenate([-sin, sin], axis=-1)


def _pad_tokens(a, batch, seq, seq_pad):
    n = a.shape[-1]
    return jnp.pad(a.reshape(batch, seq, n), ((0, 0), (0, seq_pad - seq), (0, 0))).reshape(batch * seq_pad, n)


def _unpad_tokens(a, batch, seq, seq_pad):
    n = a.shape[-1]
    return a.reshape(batch, seq_pad, n)[:, :seq].reshape(batch * seq, n)


def _layer(layer, x2, batch, seq, mod_l, pos, w, s_hgrn, s_ret, past, final):
    m, d = x2.shape
    tm = min(512, m)
    shift, scale, gate = _mod_arrays(mod_l, batch, seq, tm, d)
    u = _norm_mod(x2, w["g_norm"], shift, scale, tm, seq)

    cols = _col_offsets(d)
    w_in = w["w_in"]
    wcol = lambda i: w_in[:, cols[i][0]:cols[i][1]].astype(BF16)

    logf_a, k_a, q_a, v_a, z_a = _proj_hgrn(u, wcol(0), wcol(1), wcol(2), wcol(3), w["lb_logits"], layer, tm)
    cos, sin = _rope_tables(pos)
    if cos.shape[0] % tm != 0:
        cos = jnp.tile(cos, (m // cos.shape[0], 1))
        sin = jnp.tile(sin, (m // sin.shape[0], 1))
    q_b, k_b, v_b, z_b = _proj_ret(u, wcol(4), wcol(5), wcol(6), wcol(7), cos, sin, tm)
    q_c, k_c32, k_c16, v_c32, v_c16, z_c = _proj_fox(u, wcol(8), wcol(9), wcol(10), wcol(12), tm)
    logf_c = _proj_foxf(u, w_in[:, cols[11][0]:cols[11][1]], w["b_fox_f"], tm)
    gates = _proj_gate(u, wcol(13), tm)

    if past is None:
        y_a, s_hgrn_new = _hgrn_scan(logf_a, k_a, q_a, v_a, z_a, w["g_hgrn"], s_hgrn, batch, seq)
        y_b, s_ret_new = _ret_scan(q_b, k_b, v_b, z_b, w["g_ret"], s_ret, batch, seq,
                                   min(RET_CHUNK, seq), min(RET_CHUNK, seq))
        f_row = _fcum(logf_c.reshape(batch, seq, H_C).transpose(0, 2, 1)).reshape(batch, H_C, 1, seq)
        y_c = _fox_prompt(q_c, k_c16, v_c16, f_row, z_c, batch, seq)
    else:
        cache_k, cache_v, lf_t, page_table = past
        sp = HGRN_CHUNK
        pad = lambda a: _pad_tokens(a, batch, seq, sp)
        y_a, s_hgrn_new = _hgrn_scan(pad(logf_a), pad(k_a), pad(q_a), pad(v_a), pad(z_a),
                                     w["g_hgrn"], s_hgrn, batch, sp)
        y_a = _unpad_tokens(y_a, batch, seq, sp)
        y_b, s_ret_new = _ret_scan(pad(q_b), pad(k_b), pad(v_b), pad(z_b), w["g_ret"], s_ret,
                                   batch, sp, sp, seq)
        y_b = _unpad_tokens(y_b, batch, seq, sp)

        ps = cache_k.shape[2]
        new_pad = LANES // H_C
        hn = _new_bias(_pad_tokens(logf_c, batch, seq, ps).reshape(batch, ps, H_C))
        hn_rows = hn[:, :new_pad].reshape(batch, 1, new_pad * H_C)
        to_rows = lambda a: a.reshape(batch, seq * H_C, DH_C)
        new_rows = lambda a: _pad_tokens(a, batch, seq, new_pad).reshape(batch, new_pad * H_C, DH_C)
        y_rows = _fox_decode(to_rows(q_c), cache_k, cache_v, lf_t, new_rows(k_c16), new_rows(v_c16),
                             hn_rows, to_rows(z_c), page_table, layer, seq, new_pad)
        y_c = y_rows.reshape(m, W_C)

    bf = lambda name: w[name].astype(BF16)
    outs = _merge(y_a, y_b, y_c, gates, x2, gate, bf("w_proj_a"), bf("w_proj_b"), bf("w_proj_c"),
                  bf("w_out"), w["g_final"], min(256, m), seq, final)
    return outs, s_hgrn_new, s_ret_new, k_c32, v_c32, logf_c


def kernel(x_prompt, x_sample, c_prompt, c_sample, cache_k, cache_v, cache_logf, state_hgrn, state_ret,
           page_table, w_ada, b_ada, g_norm, w_in, b_fox_f, lb_logits, g_hgrn, g_ret, w_proj_a, w_proj_b,
           w_proj_c, w_out, g_final):
    bp, lp, d = x_prompt.shape
    bs, ls, _ = x_sample.shape
    depth = w_in.shape[0]
    n_pages = page_table.shape[1]
    past_len = n_pages * cache_k.shape[2]
    pos_p = jnp.arange(lp, dtype=jnp.int32)
    pos_s = past_len + jnp.arange(ls, dtype=jnp.int32)

    n_c = bp + bs
    rows = -(-n_c // 8) * 8
    c_all = jnp.pad(jnp.concatenate([c_prompt, c_sample], axis=0), ((0, rows - n_c), (0, 0)))
    mod = _ada_mod(c_all, w_ada, b_ada)

    lf_t = jnp.transpose(cache_logf, (1, 0, 3, 2))
    zero_hgrn = jnp.zeros((bp, H_A, DK_A, DV_A), F32)
    zero_ret = jnp.zeros((bp, H_B, DK_B, DV_B), F32)
    xp = x_prompt.reshape(bp * lp, d)
    xs = x_sample.reshape(bs * ls, d)
    acc = {k: [] for k in ("kp", "vp", "fp", "hp", "rp", "ks", "vs", "fs", "hs", "rs")}
    yp = ys = None
    for l in range(depth):
        w = dict(g_norm=g_norm[l], w_in=w_in[l], b_fox_f=b_fox_f[l], lb_logits=lb_logits,
                 g_hgrn=g_hgrn[l], g_ret=g_ret[l], w_proj_a=w_proj_a[l], w_proj_b=w_proj_b[l],
                 w_proj_c=w_proj_c[l], w_out=w_out[l], g_final=g_final)
        final = l == depth - 1
        outs, h, r, k, v, f = _layer(l, xp, bp, lp, mod[l, :bp], pos_p, w, zero_hgrn, zero_ret, None, final)
        xp = outs[0]
        if final:
            yp = outs[1]
        acc["kp"].append(k.reshape(bp, lp, H_C, DH_C)); acc["vp"].append(v.reshape(bp, lp, H_C, DH_C))
        acc["fp"].append(f.reshape(bp, lp, H_C)); acc["hp"].append(h); acc["rp"].append(r)

        outs, h, r, k, v, f = _layer(l, xs, bs, ls, mod[l, bp:bp + bs], pos_s, w, state_hgrn[:, l],
                                     state_ret[:, l], (cache_k, cache_v, lf_t, page_table), final)
        xs = outs[0]
        if final:
            ys = outs[1]
        acc["ks"].append(k.reshape(bs, ls, H_C, DH_C)); acc["vs"].append(v.reshape(bs, ls, H_C, DH_C))
        acc["fs"].append(f.reshape(bs, ls, H_C)); acc["hs"].append(h); acc["rs"].append(r)

    st = lambda name: jnp.stack(acc[name], axis=1)
    return (yp.reshape(bp, lp, d), ys.reshape(bs, ls, d),
            st("kp"), st("vp"), st("fp"), st("hp"), st("rp"),
            st("ks"), st("vs"), st("fs"), st("hs"), st("rs"))
```

```python
import functools
import math

import numpy as np
import jax
import jax.numpy as jnp
from jax import lax
from jax.experimental import pallas as pl
from jax.experimental.pallas import tpu as pltpu

F32 = jnp.float32
BF16 = jnp.bfloat16

H_A, DK_A, DV_A = 8, 128, 128
H_B, DK_B, DV_B = 4, 128, 256
H_C, DH_C = 8, 128
W_A = H_A * DV_A
W_B = H_B * DV_B
W_C = H_C * DH_C
N_BRANCH = 3
ROPE_BASE = 10000.0
RET_DECAY_BASE = 5.0
NORM_EPS = 1e-6
MASK_VALUE = -1e30
LOG2E = math.log2(math.e)

LANES = 128
HGRN_CHUNK = 128
HGRN_HEADS_PER_STEP = 8
PROJ_ROWS = 2048
DECODE_PAGES_PER_STEP = 8
RET_CHUNK = 256
FOX_BLOCK = 512
FOX_SOFTMAX_ROWS = 128
VMEM_LIMIT = 56 * 1024 * 1024


def _cparams(sem):
    return pltpu.CompilerParams(dimension_semantics=sem, vmem_limit_bytes=VMEM_LIMIT)


def _dot(a, b):
    return jnp.dot(a, b, preferred_element_type=F32)


def _dot_nt(a, b):
    return lax.dot_general(a, b, (((1,), (1,)), ((), ())), preferred_element_type=F32)


def _split3(x):
    hi = x.astype(BF16)
    r = x - hi.astype(F32)
    mid = r.astype(BF16)
    lo = (r - mid.astype(F32)).astype(BF16)
    return hi, mid, lo


def _dot_exact_lhs(t, x, terms=3):
    parts = _split3(x)[:terms]
    out = _dot(t, parts[0])
    for part in parts[1:]:
        out = out + _dot(t, part)
    return out


def _dot_exact_rhs(x, t):
    hi, mid, lo = _split3(x)
    return _dot(hi, t) + _dot(mid, t) + _dot(lo, t)


def _sigmoid(x):
    return 1.0 / (1.0 + jnp.exp(-x))


def _silu(x):
    return x * _sigmoid(x)


def _log_sigmoid(x):
    return jnp.minimum(x, 0.0) - jnp.log(1.0 + jnp.exp(-jnp.abs(x)))


def _ada_body(c_ref, w_ref, b_ref, o_ref):
    c = c_ref[...]
    w = w_ref[0]
    c_hi = c.astype(BF16)
    c_lo = (c - c_hi.astype(F32)).astype(BF16)
    w_hi = w.astype(BF16)
    w_lo = (w - w_hi.astype(F32)).astype(BF16)
    o_ref[0] = _dot(c_hi, w_hi) + _dot(c_hi, w_lo) + _dot(c_lo, w_hi) + b_ref[0]


def _ada_mod(c_all, w_ada, b_ada):
    depth, d, d3 = w_ada.shape
    rows = c_all.shape[0]
    return pl.pallas_call(
        _ada_body,
        grid=(depth, d3 // d),
        in_specs=[pl.BlockSpec((rows, d), lambda l, j: (0, 0)),
                  pl.BlockSpec((1, d, d), lambda l, j: (l, 0, j)),
                  pl.BlockSpec((1, 1, d), lambda l, j: (l, 0, j))],
        out_specs=pl.BlockSpec((1, rows, d), lambda l, j: (l, 0, j)),
        out_shape=jax.ShapeDtypeStruct((depth, rows, d3), F32),
        compiler_params=_cparams(("arbitrary", "arbitrary")),
        name="ada_mod",
    )(c_all, w_ada, b_ada.reshape(depth, 1, d3))


def _mod_spec(mod, tm, seq):
    if mod.ndim == 3:
        return pl.BlockSpec((None, 1, mod.shape[-1]), lambda i: ((i * tm) // seq, 0, 0))
    return pl.BlockSpec((tm, mod.shape[-1]), lambda i: (i, 0))


def _mod_arrays(mod_l, batch, seq, tm, d):
    parts = [mod_l[:, k * d:(k + 1) * d] for k in range(3)]
    if seq % tm == 0:
        return [p.reshape(batch, 1, d) for p in parts]
    return [jnp.broadcast_to(p[:, None, :], (batch, seq, d)).reshape(batch * seq, d) for p in parts]


def _norm_body(x_ref, g_ref, sh_ref, sc_ref, u_ref):
    x = x_ref[...]
    ms = jnp.mean(x * x, axis=-1, keepdims=True)
    y = x * lax.rsqrt(ms + NORM_EPS) * g_ref[...]
    u_ref[...] = (y * (1.0 + sc_ref[...]) + sh_ref[...]).astype(BF16)


def _norm_mod(x2, g, shift, scale, tm, seq):
    m, d = x2.shape
    return pl.pallas_call(
        _norm_body,
        grid=(m // tm,),
        in_specs=[pl.BlockSpec((tm, d), lambda i: (i, 0)),
                  pl.BlockSpec((1, d), lambda i: (0, 0)),
                  _mod_spec(shift, tm, seq), _mod_spec(scale, tm, seq)],
        out_specs=pl.BlockSpec((tm, d), lambda i: (i, 0)),
        out_shape=jax.ShapeDtypeStruct((m, d), BF16),
        compiler_params=_cparams(("parallel",)),
        name="norm_mod",
    )(x2, g.reshape(1, d), shift, scale)


def _proj_hgrn_body(layer, u_ref, wf_ref, wq_ref, wi_ref, wz_ref, lbl_ref,
                    logf_ref, k_ref, q_ref, v_ref, z_ref):
    u = u_ref[...]
    lbl = lbl_ref[...]
    e = jnp.exp(lbl - jnp.max(lbl, axis=0, keepdims=True))
    sm = e / jnp.sum(e, axis=0, keepdims=True)
    lb = jnp.zeros_like(sm[0:1])
    for r in range(1, layer + 1):
        lb = lb + sm[r:r + 1]
    fl = _dot(u, wf_ref[...])
    logf_ref[...] = jnp.log(lb + (1.0 - lb) * _sigmoid(fl))
    k_ref[...] = (1.0 - lb) * _sigmoid(-fl)
    q_ref[...] = _silu(_dot(u, wq_ref[...])) * (DK_A ** -0.5)
    v_ref[...] = _dot(u, wi_ref[...]).astype(BF16)
    z_ref[...] = _silu(_dot(u, wz_ref[...])).astype(BF16)


def _proj_hgrn(u, wf, wq, wi, wz, lb_logits, layer, tm, tn=256):
    m, d = u.shape
    n = wf.shape[1]
    depth = lb_logits.shape[0]
    wspec = pl.BlockSpec((d, tn), lambda j, i: (0, j))
    ospec = pl.BlockSpec((tm, tn), lambda j, i: (i, j))
    return pl.pallas_call(
        functools.partial(_proj_hgrn_body, layer),
        grid=(n // tn, m // tm),
        in_specs=[pl.BlockSpec((tm, d), lambda j, i: (i, 0)), wspec, wspec, wspec, wspec,
                  pl.BlockSpec((depth, tn), lambda j, i: (0, j))],
        out_specs=[ospec] * 5,
        out_shape=[jax.ShapeDtypeStruct((m, n), F32), jax.ShapeDtypeStruct((m, n), F32),
                   jax.ShapeDtypeStruct((m, n), F32), jax.ShapeDtypeStruct((m, n), BF16),
                   jax.ShapeDtypeStruct((m, n), BF16)],
        compiler_params=_cparams(("parallel", "parallel")),
        name="proj_hgrn",
    )(u, wf, wq, wi, wz, lb_logits)


def _rope(x, cos, sin_signed):
    return x * cos + pltpu.roll(x, DK_B // 2, axis=1) * sin_signed


def _proj_ret_body(u_ref, wq_ref, wk_ref, wv_ref, wz_ref, cos_ref, sin_ref,
                   q_ref, k_ref, v_ref, z_ref):
    u = u_ref[...]
    cos = cos_ref[...]
    sin = sin_ref[...]
    q_ref[...] = _rope(_dot(u, wq_ref[...]), cos, sin) * (DK_B ** -0.5)
    k_ref[...] = _rope(_dot(u, wk_ref[...]), cos, sin)
    v_ref[...] = _dot(u, wv_ref[...]).astype(BF16)
    z_ref[...] = _silu(_dot(u, wz_ref[...])).astype(BF16)


def _proj_ret(u, wq, wk, wv, wz, cos, sin, tm):
    m, d = u.shape
    n_rope_tiles = cos.shape[0] // tm
    qk_w = pl.BlockSpec((d, DK_B), lambda h, i: (0, h))
    vz_w = pl.BlockSpec((d, DV_B), lambda h, i: (0, h))
    rope_spec = pl.BlockSpec((tm, DK_B), lambda h, i: (i % n_rope_tiles, 0))
    qk_o = pl.BlockSpec((tm, DK_B), lambda h, i: (i, h))
    vz_o = pl.BlockSpec((tm, DV_B), lambda h, i: (i, h))
    return pl.pallas_call(
        _proj_ret_body,
        grid=(H_B, m // tm),
        in_specs=[pl.BlockSpec((tm, d), lambda h, i: (i, 0)), qk_w, qk_w, vz_w, vz_w,
                  rope_spec, rope_spec],
        out_specs=[qk_o, qk_o, vz_o, vz_o],
        out_shape=[jax.ShapeDtypeStruct((m, H_B * DK_B), F32), jax.ShapeDtypeStruct((m, H_B * DK_B), F32),
                   jax.ShapeDtypeStruct((m, W_B), BF16), jax.ShapeDtypeStruct((m, W_B), BF16)],
        compiler_params=_cparams(("parallel", "parallel")),
        name="proj_ret",
    )(u, wq, wk, wv, wz, cos, sin)


def _proj_fox_body(u_ref, wq_ref, wk_ref, wv_ref, wz_ref,
                   q16_ref, k32_ref, k16_ref, v32_ref, v16_ref, z_ref):
    u = u_ref[...]
    q16_ref[...] = _dot(u, wq_ref[...]).astype(BF16)
    k = _dot(u, wk_ref[...])
    k32_ref[...] = k
    k16_ref[...] = k.astype(BF16)
    v = _dot(u, wv_ref[...])
    v32_ref[...] = v
    v16_ref[...] = v.astype(BF16)
    z_ref[...] = _silu(_dot(u, wz_ref[...])).astype(BF16)


def _proj_fox(u, wq, wk, wv, wz, tm, tn=256):
    m, d = u.shape
    n = wq.shape[1]
    wspec = pl.BlockSpec((d, tn), lambda j, i: (0, j))
    ospec = pl.BlockSpec((tm, tn), lambda j, i: (i, j))
    shp = lambda dt: jax.ShapeDtypeStruct((m, n), dt)
    return pl.pallas_call(
        _proj_fox_body,
        grid=(n // tn, m // tm),
        in_specs=[pl.BlockSpec((tm, d), lambda j, i: (i, 0)), wspec, wspec, wspec, wspec],
        out_specs=[ospec] * 6,
        out_shape=[shp(BF16), shp(F32), shp(BF16), shp(F32), shp(BF16), shp(BF16)],
        compiler_params=_cparams(("parallel", "parallel")),
        name="proj_fox",
    )(u, wq, wk, wv, wz)


def _proj_foxf_body(u_ref, whi_ref, wlo_ref, b_ref, o_ref):
    u = u_ref[...]
    cf = _dot(u, whi_ref[...]) + _dot(u, wlo_ref[...]) + b_ref[...]
    o_ref[...] = _log_sigmoid(cf)[:, :H_C]


def _proj_foxf(u, wf, bf, tm):
    m, d = u.shape
    wpad = jnp.pad(wf, ((0, 0), (0, LANES - H_C)))
    whi = wpad.astype(BF16)
    wlo = (wpad - whi.astype(F32)).astype(BF16)
    bpad = jnp.pad(bf.reshape(1, H_C), ((0, 0), (0, LANES - H_C)))
    wspec = pl.BlockSpec((d, LANES), lambda i: (0, 0))
    return pl.pallas_call(
        _proj_foxf_body,
        grid=(m // tm,),
        in_specs=[pl.BlockSpec((tm, d), lambda i: (i, 0)), wspec, wspec,
                  pl.BlockSpec((1, LANES), lambda i: (0, 0))],
        out_specs=pl.BlockSpec((tm, H_C), lambda i: (i, 0)),
        out_shape=jax.ShapeDtypeStruct((m, H_C), F32),
        compiler_params=_cparams(("parallel",)),
        name="proj_foxf",
    )(u, whi, wlo, bpad)


def _proj_gate_body(u_ref, w_ref, o_ref):
    o_ref[...] = _sigmoid(_dot(u_ref[...], w_ref[...])).astype(BF16)


def _proj_gate(u, wg, tm, tn=512):
    m, d = u.shape
    n = wg.shape[1]
    return pl.pallas_call(
        _proj_gate_body,
        grid=(n // tn, m // tm),
        in_specs=[pl.BlockSpec((tm, d), lambda j, i: (i, 0)),
                  pl.BlockSpec((d, tn), lambda j, i: (0, j))],
        out_specs=pl.BlockSpec((tm, tn), lambda j, i: (i, j)),
        out_shape=jax.ShapeDtypeStruct((m, n), BF16),
        compiler_params=_cparams(("parallel", "parallel")),
        name="proj_gate",
    )(u, wg)


def _hgrn_levels(c):
    return [c >> (i + 1) for i in range(int(math.log2(c)))]


def _hgrn_tables(c):
    t = np.arange(c)[:, None]
    u = np.arange(c)[None, :]
    mats, masks = [], []
    for w in _hgrn_levels(c):
        r = (t // (2 * w)) * (2 * w) + w
        upper = (t % (2 * w)) >= w
        mats.append(np.where(upper, (u >= r) & (u <= t), (u > t) & (u < r)))
        masks.append((t // (2 * w)) == (u // (2 * w)))
    mats.append(u <= t)
    mats.append(u > t)
    masks.append(t == u)
    return (jnp.asarray(np.concatenate(mats, 0).astype(np.float32), BF16),
            jnp.asarray(np.stack(masks, 0).astype(np.float32), F32))


def _hgrn_body(c, nc, hp, logf_ref, k_ref, q_ref, v_ref, z_ref, g_ref, s0_ref, t_ref, mask_ref,
               y_ref, sout_ref, st_ref):
    ci = pl.program_id(2)

    @pl.when(ci == 0)
    def _():
        for hh in range(hp):
            st_ref[hh] = s0_ref[0, hh].T

    levels = _hgrn_levels(c)
    nl = len(levels)
    e_all = jnp.exp(_dot_exact_lhs(t_ref[...], logf_ref[...], terms=2))
    row = lax.broadcasted_iota(jnp.int32, (c, DK_A), 0)
    g = g_ref[...]
    for hh in range(hp):
        cols = slice(hh * DK_A, (hh + 1) * DK_A)
        q = q_ref[:, cols]
        k = k_ref[:, cols]
        v = v_ref[:, cols]
        a = mask_ref[nl] * _dot_nt(q.astype(BF16), k.astype(BF16))
        for li, w in enumerate(levels):
            e_w = e_all[li * c:(li + 1) * c, cols]
            upper = (row & w) != 0
            q_w = jnp.where(upper, q * e_w, 0.0).astype(BF16)
            k_w = jnp.where(upper, 0.0, k * e_w).astype(BF16)
            a = a + mask_ref[li] * _dot_nt(q_w, k_w)
        e_b = e_all[nl * c:(nl + 1) * c, cols]
        e_end = e_all[(nl + 1) * c:(nl + 2) * c, cols]
        st = st_ref[hh]
        o = _dot(a.astype(BF16), v) + _dot_nt((q * e_b).astype(BF16), st.astype(BF16))
        v_t = v.astype(F32).T.astype(BF16)
        st_ref[hh] = st * e_b[c - 1:c, :] + _dot(v_t, (k * e_end).astype(BF16))
        ms = jnp.mean(o * o, axis=-1, keepdims=True)
        y_ref[:, cols] = (o * lax.rsqrt(ms + NORM_EPS) * g * z_ref[:, cols]).astype(BF16)

    @pl.when(ci == nc - 1)
    def _():
        for hh in range(hp):
            sout_ref[0, hh] = st_ref[hh].T


def _hgrn_scan(logf, k, q, v, z, g, s0, batch, seq):
    c = HGRN_CHUNK
    hp = HGRN_HEADS_PER_STEP
    nc = seq // c
    m = batch * seq
    tmat, masks = _hgrn_tables(c)
    tok = pl.BlockSpec((c, hp * DK_A), lambda b, h, ci: (b * nc + ci, h))
    st_spec = pl.BlockSpec((1, hp, DK_A, DV_A), lambda b, h, ci: (b, h, 0, 0))
    return pl.pallas_call(
        functools.partial(_hgrn_body, c, nc, hp),
        grid=(batch, H_A // hp, nc),
        in_specs=[tok, tok, tok, tok, tok,
                  pl.BlockSpec((1, DV_A), lambda b, h, ci: (0, 0)),
                  st_spec,
                  pl.BlockSpec(tmat.shape, lambda b, h, ci: (0, 0)),
                  pl.BlockSpec(masks.shape, lambda b, h, ci: (0, 0, 0))],
        out_specs=[tok, st_spec],
        out_shape=[jax.ShapeDtypeStruct((m, W_A), BF16),
                   jax.ShapeDtypeStruct((batch, H_A, DK_A, DV_A), F32)],
        scratch_shapes=[pltpu.VMEM((hp, DV_A, DK_A), F32)],
        compiler_params=_cparams(("parallel", "parallel", "arbitrary")),
        name="hgrn_scan",
    )(logf, k, q, v, z, g.reshape(1, DV_A), s0, tmat, masks)


def _ret_body(c, c_eff, nc, lg_ref, q_ref, k_ref, v_ref, z_ref, g_ref, s0_ref,
              y_ref, sout_ref, s_ref):
    ci = pl.program_id(1)

    @pl.when(ci == 0)
    def _():
        s_ref[...] = s0_ref[0]

    t = lax.broadcasted_iota(jnp.int32, (c, DK_B), 0).astype(F32)
    rel = (lax.broadcasted_iota(jnp.int32, (c, c), 0)
           - lax.broadcasted_iota(jnp.int32, (c, c), 1)).astype(F32)
    g = g_ref[...]
    for h in range(H_B):
        qk_cols = slice(h * DK_B, (h + 1) * DK_B)
        vz_cols = slice(h * DV_B, (h + 1) * DV_B)
        lg = lg_ref[h]
        q = q_ref[:, qk_cols]
        k = k_ref[:, qk_cols]
        v = v_ref[:, vz_cols]
        q_dec = jnp.exp((t + 1.0) * lg)
        k_dec = jnp.where(t < c_eff, jnp.exp(jnp.maximum(c_eff - 1.0 - t, 0.0) * lg), 0.0)
        intra = jnp.where(rel >= 0.0, jnp.exp(jnp.maximum(rel, 0.0) * lg), 0.0)
        a = _dot_nt(q.astype(BF16), k.astype(BF16)) * intra
        s = s_ref[h]
        o = _dot(a.astype(BF16), v) + _dot((q * q_dec).astype(BF16), s.astype(BF16))
        kd_t = (k * k_dec).T.astype(BF16)
        chunk_dec = jnp.exp(jnp.full((1, DV_B), c_eff, F32) * lg)
        s_ref[h] = chunk_dec * s + _dot(kd_t, v)
        ms = jnp.mean(o * o, axis=-1, keepdims=True)
        y_ref[:, vz_cols] = (o * lax.rsqrt(ms + NORM_EPS) * g * z_ref[:, vz_cols]).astype(BF16)

    @pl.when(ci == nc - 1)
    def _():
        sout_ref[0] = s_ref[...]


def _ret_scan(q, k, v, z, g, s0, batch, seq, c, c_eff):
    nc = seq // c
    m = batch * seq
    log_gamma = jnp.log1p(-jnp.power(2.0, -RET_DECAY_BASE - jnp.arange(H_B, dtype=F32)))
    qk = pl.BlockSpec((c, H_B * DK_B), lambda b, ci: (b * nc + ci, 0))
    vz = pl.BlockSpec((c, W_B), lambda b, ci: (b * nc + ci, 0))
    st_spec = pl.BlockSpec((1, H_B, DK_B, DV_B), lambda b, ci: (b, 0, 0, 0))
    return pl.pallas_call(
        functools.partial(_ret_body, c, c_eff, nc),
        grid=(batch, nc),
        in_specs=[pl.BlockSpec(memory_space=pltpu.SMEM), qk, qk, vz, vz,
                  pl.BlockSpec((1, DV_B), lambda b, ci: (0, 0)), st_spec],
        out_specs=[vz, st_spec],
        out_shape=[jax.ShapeDtypeStruct((m, W_B), BF16),
                   jax.ShapeDtypeStruct((batch, H_B, DK_B, DV_B), F32)],
        scratch_shapes=[pltpu.VMEM((H_B, DK_B, DV_B), F32)],
        compiler_params=_cparams(("parallel", "arbitrary")),
        name="ret_scan",
    )(log_gamma, q, k, v, z, g.reshape(1, DV_B), s0)


def _fcum_body(nblk, x_ref, o_ref):
    r = lax.broadcasted_iota(jnp.int32, (LANES, LANES), 0)
    cc = lax.broadcasted_iota(jnp.int32, (LANES, LANES), 1)
    tri = jnp.where(r <= cc, 1.0, 0.0).astype(BF16)
    carry = jnp.zeros((H_C, 1), F32)
    for blk in range(nblk):
        x = x_ref[0, :, blk * LANES:(blk + 1) * LANES]
        y = _dot_exact_rhs(x, tri) + carry
        o_ref[0, :, blk * LANES:(blk + 1) * LANES] = y * LOG2E
        carry = y[:, LANES - 1:LANES]


def _fcum(logf_t):
    b, h, seq = logf_t.shape
    spec = pl.BlockSpec((1, h, seq), lambda i: (i, 0, 0))
    return pl.pallas_call(
        functools.partial(_fcum_body, seq // LANES),
        grid=(b,),
        in_specs=[spec], out_specs=spec,
        out_shape=jax.ShapeDtypeStruct((b, h, seq), F32),
        compiler_params=_cparams(("parallel",)),
        name="fox_fcum",
    )(logf_t)


def _fold_lanes(x, op):
    out = x[:, 0:LANES]
    for c in range(1, x.shape[1] // LANES):
        out = op(out, x[:, c * LANES:(c + 1) * LANES])
    return out


def _fox_prompt_body(tq, q_ref, k_ref, v_ref, fq_ref, fk_ref, z_ref, o_ref, fqc_ref, t_ref, p_ref, st_ref):
    i = pl.program_id(2)
    q = (q_ref[...].astype(F32) * (DH_C ** -0.5 * LOG2E)).astype(BF16)
    fqc_ref[...] = jnp.broadcast_to(fq_ref[0, 0], (LANES, tq)).T
    m_ref, l_ref, a_ref, acc_ref = st_ref.at[0], st_ref.at[1], st_ref.at[2], st_ref.at[3]
    m_ref[...] = jnp.full((tq, LANES), MASK_VALUE, F32)
    l_ref[...] = jnp.zeros((tq, LANES), F32)
    a_ref[...] = jnp.ones((tq, LANES), F32)
    acc_ref[...] = jnp.zeros((tq, DH_C), F32)

    def scores(j):
        start = pl.multiple_of(j * tq, tq)
        return _dot_nt(q, k_ref[pl.ds(start, tq), :]) - fk_ref[0, 0, :, pl.ds(start, tq)]

    def softmax(slot, diagonal):
        for r in range(tq // FOX_SOFTMAX_ROWS):
            rows = slice(r * FOX_SOFTMAX_ROWS, (r + 1) * FOX_SOFTMAX_ROWS)
            t = t_ref[slot, rows, :]
            if diagonal:
                rr = lax.broadcasted_iota(jnp.int32, t.shape, 0) + r * FOX_SOFTMAX_ROWS
                cc = lax.broadcasted_iota(jnp.int32, t.shape, 1)
                t = jnp.where(rr >= cc, t, MASK_VALUE)
            fq = fqc_ref[rows, :]
            m_prev = m_ref[rows, :]
            m_new = jnp.maximum(m_prev, jnp.max(_fold_lanes(t, jnp.maximum), axis=-1, keepdims=True) + fq)
            p = jnp.exp2(t + jnp.concatenate([fq - m_new] * (tq // LANES), axis=1))
            alpha = jnp.exp2(m_prev - m_new)
            l_ref[rows, :] = alpha * l_ref[rows, :] + jnp.sum(_fold_lanes(p, jnp.add), axis=-1, keepdims=True)
            a_ref[rows, :] = alpha
            m_ref[rows, :] = m_new
            p_ref[slot, rows, :] = p.astype(BF16)

    def weighted(j, slot):
        start = pl.multiple_of(j * tq, tq)
        acc_ref[...] = a_ref[...] * acc_ref[...] + _dot(p_ref[slot], v_ref[pl.ds(start, tq), :])

    t_ref[0] = scores(0)
    p_ref[1] = jnp.zeros((tq, tq), BF16)

    def trip(j, carry):
        slot = j % 2
        weighted(jnp.maximum(j - 1, 0), 1 - slot)
        softmax(slot, False)
        t_ref[1 - slot] = scores(j + 1)
        return carry

    lax.fori_loop(0, i, trip, 0)
    slot = i % 2
    weighted(jnp.maximum(i - 1, 0), 1 - slot)
    softmax(slot, True)
    weighted(i, slot)
    o_ref[...] = (acc_ref[...] / l_ref[...] * z_ref[...]).astype(BF16)


def _fox_prompt(q16, k16, v16, f_row, z, batch, seq):
    tq = min(FOX_BLOCK, seq)
    nq = seq // tq
    m = batch * seq
    qspec = pl.BlockSpec((tq, DH_C), lambda b, h, i: (b * nq + i, h))
    kvspec = pl.BlockSpec((seq, DH_C), lambda b, h, i: (b, h))
    return pl.pallas_call(
        functools.partial(_fox_prompt_body, tq),
        grid=(batch, H_C, nq),
        in_specs=[qspec, kvspec, kvspec,
                  pl.BlockSpec((1, 1, 1, tq), lambda b, h, i: (b, h, 0, i)),
                  pl.BlockSpec((1, 1, 1, seq), lambda b, h, i: (b, h, 0, 0)),
                  qspec],
        out_specs=qspec,
        out_shape=jax.ShapeDtypeStruct((m, W_C), BF16),
        scratch_shapes=[pltpu.VMEM((tq, LANES), F32), pltpu.VMEM((2, tq, tq), F32),
                        pltpu.VMEM((2, tq, tq), BF16), pltpu.VMEM((4, tq, LANES), F32)],
        compiler_params=_cparams(("parallel", "parallel", "arbitrary")),
        name="fox_prompt",
    )(q16, k16, v16, f_row, f_row, z)


def _tri_f32(n, kind):
    r = lax.broadcasted_iota(jnp.int32, (n, n), 0)
    c = lax.broadcasted_iota(jnp.int32, (n, n), 1)
    cond = (c > r) if kind == "after" else (c <= r)
    return jnp.where(cond, 1.0, 0.0).astype(BF16)


def _new_bias_body(n, x_ref, o_ref):
    o_ref[0] = -_dot_exact_lhs(_tri_f32(n, "upto"), x_ref[0])


def _new_bias(logf_new_pad):
    nb, n, h = logf_new_pad.shape
    spec = pl.BlockSpec((1, n, h), lambda b: (b, 0, 0))
    return pl.pallas_call(
        functools.partial(_new_bias_body, n),
        grid=(nb,),
        in_specs=[spec], out_specs=spec,
        out_shape=jax.ShapeDtypeStruct((nb, n, h), F32),
        compiler_params=_cparams(("parallel",)),
        name="fox_new_bias",
    )(logf_new_pad)


def _fox_decode_body(n_steps, pg, ps, n_new, new_pad, pt_ref, q_ref, *refs):
    k_refs = refs[:pg]
    v_refs = refs[pg:2 * pg]
    lf_refs = refs[2 * pg:3 * pg]
    rep_ref, kn_ref, vn_ref, hn_ref, z_ref, o_ref, m_ref, l_ref, acc_ref, carry_ref = refs[3 * pg:]
    step = pl.program_id(1)
    scale = DH_C ** -0.5
    rows = H_C * n_new
    cols = ps * H_C

    @pl.when(step == 0)
    def _():
        m_ref[...] = jnp.full_like(m_ref, MASK_VALUE)
        l_ref[...] = jnp.zeros_like(l_ref)
        acc_ref[...] = jnp.zeros_like(acc_ref)
        carry_ref[...] = jnp.zeros_like(carry_ref)

    @pl.when(step < n_steps)
    def _():
        lf = jnp.concatenate([r[0, 0] for r in lf_refs], axis=0)
        t_from = lax.broadcasted_iota(jnp.int32, (ps, ps), 0)
        t_to = lax.broadcasted_iota(jnp.int32, (ps, ps), 1)
        later = jnp.where(t_from > t_to, 1.0, 0.0).astype(BF16)
        within = _dot_exact_rhs(lf, later)
        total = jnp.sum(lf, axis=-1, keepdims=True)
        carry = carry_ref[:, 0:1]
        bias = []
        for i in range(pg):
            bias.append(within[i * H_C:(i + 1) * H_C] + carry)
            carry = carry + total[i * H_C:(i + 1) * H_C]
        carry_ref[...] = jnp.broadcast_to(carry, carry_ref.shape)
        spread = _dot_exact_rhs(jnp.concatenate(bias, axis=0), rep_ref[...])
        other_head = jnp.where(lax.broadcasted_iota(jnp.int32, (H_C, cols), 0)
                               == lax.broadcasted_iota(jnp.int32, (H_C, cols), 1) % H_C, 0.0, MASK_VALUE)

        q = q_ref[0]
        s_list = []
        for i in range(pg):
            kf = k_refs[i][0, 0].reshape(cols, DH_C).astype(BF16)
            b8 = spread[i * H_C:(i + 1) * H_C] + other_head
            s_list.append(_dot_nt(q, kf) * scale + jnp.concatenate([b8] * n_new, axis=0))
        mx = s_list[0]
        for s in s_list[1:]:
            mx = jnp.maximum(mx, s)
        m_prev = m_ref[:, 0:1]
        m_new = jnp.maximum(m_prev, jnp.max(_fold_lanes(mx, jnp.maximum), axis=-1, keepdims=True))
        alpha = jnp.exp(m_prev - m_new)
        psum = None
        pv = None
        for i in range(pg):
            p = jnp.exp(s_list[i] - m_new)
            contrib = _dot(p.astype(BF16), v_refs[i][0, 0].reshape(cols, DH_C).astype(BF16))
            psum = p if psum is None else psum + p
            pv = contrib if pv is None else pv + contrib
        l_ref[...] = alpha * l_ref[...] + jnp.sum(_fold_lanes(psum, jnp.add), axis=-1, keepdims=True)
        acc_ref[...] = alpha * acc_ref[...] + pv
        m_ref[...] = jnp.broadcast_to(m_new, m_ref.shape)

    @pl.when(step == n_steps)
    def _():
        ncol = new_pad * H_C
        s = _dot_nt(q_ref[0], kn_ref[0]) * scale + hn_ref[0]
        rr = lax.broadcasted_iota(jnp.int32, (rows, ncol), 0)
        cc = lax.broadcasted_iota(jnp.int32, (rows, ncol), 1)
        tok = cc // H_C
        ok = ((rr % H_C) == (cc % H_C)) & (tok <= (rr // H_C)) & (tok < n_new)
        s = jnp.where(ok, s, MASK_VALUE)
        m_prev = m_ref[:, 0:1]
        m_new = jnp.maximum(m_prev, jnp.max(s, axis=-1, keepdims=True))
        p = jnp.exp(s - m_new)
        alpha = jnp.exp(m_prev - m_new)
        l_fin = alpha * l_ref[:, 0:1] + jnp.sum(p, axis=-1, keepdims=True)
        acc = alpha * acc_ref[...] + _dot(p.astype(BF16), vn_ref[0])
        o_ref[0] = (acc / l_fin * z_ref[0]).astype(BF16)


def _fox_decode(q_rows, cache_k, cache_v, lf_t, k_new, v_new, hn_rows, z_rows, page_table, layer,
                n_new, new_pad):
    nb, n_pages = page_table.shape
    ps = cache_k.shape[2]
    pg = min(DECODE_PAGES_PER_STEP, n_pages)
    n_steps = n_pages // pg
    rows = H_C * n_new
    cols = ps * H_C
    ncol = new_pad * H_C
    rep = jnp.asarray((np.arange(ps)[:, None] == np.arange(cols)[None, :] // H_C).astype(np.float32), BF16)

    def page_of(b, s, pt, i):
        return pt[b, n_pages - 1 - (jnp.minimum(s, n_steps - 1) * pg + i)]

    def kv_spec(i):
        return pl.BlockSpec((1, 1, ps, H_C, DH_C), lambda b, s, pt: (page_of(b, s, pt, i), layer, 0, 0, 0))

    def lf_spec(i):
        return pl.BlockSpec((1, 1, H_C, ps), lambda b, s, pt: (layer, page_of(b, s, pt, i), 0, 0))

    per_b = lambda shape: pl.BlockSpec((1,) + shape, lambda b, s, pt: (b, 0, 0))
    grid_spec = pltpu.PrefetchScalarGridSpec(
        num_scalar_prefetch=1,
        grid=(nb, n_steps + 1),
        in_specs=([per_b((rows, DH_C))] + [kv_spec(i) for i in range(pg)] + [kv_spec(i) for i in range(pg)]
                  + [lf_spec(i) for i in range(pg)]
                  + [pl.BlockSpec((ps, cols), lambda b, s, pt: (0, 0))]
                  + [per_b((ncol, DH_C)), per_b((ncol, DH_C)), per_b((1, ncol)), per_b((rows, DH_C))]),
        out_specs=per_b((rows, DH_C)),
        scratch_shapes=[pltpu.VMEM((rows, LANES), F32), pltpu.VMEM((rows, LANES), F32),
                        pltpu.VMEM((rows, DH_C), F32), pltpu.VMEM((H_C, LANES), F32)],
    )
    return pl.pallas_call(
        functools.partial(_fox_decode_body, n_steps, pg, ps, n_new, new_pad),
        grid_spec=grid_spec,
        out_shape=jax.ShapeDtypeStruct((nb, rows, DH_C), BF16),
        compiler_params=_cparams(("parallel", "arbitrary")),
        name="fox_decode",
    )(page_table, q_rows, *([cache_k] * pg), *([cache_v] * pg), *([lf_t] * pg), rep,
      k_new, v_new, hn_rows, z_rows)


def _merge_body(final, ya_ref, yb_ref, yc_ref, g0_ref, g1_ref, g2_ref, x_ref, gate_ref,
                wa_ref, wb_ref, wc_ref, wo_ref, gf_ref, xo_ref, *maybe_y):
    merged = (g0_ref[...] * _dot(ya_ref[...], wa_ref[...])
              + g1_ref[...] * _dot(yb_ref[...], wb_ref[...])
              + g2_ref[...] * _dot(yc_ref[...], wc_ref[...]))
    xn = x_ref[...] + gate_ref[...] * _dot(merged.astype(BF16), wo_ref[...])
    xo_ref[...] = xn
    if final:
        ms = jnp.mean(xn * xn, axis=-1, keepdims=True)
        maybe_y[0][...] = xn * lax.rsqrt(ms + NORM_EPS) * gf_ref[...]


def _merge(ya, yb, yc, g, x2, gate, wa, wb, wc, wo, g_final, tm, seq, final):
    m, d = x2.shape
    row = pl.BlockSpec((tm, d), lambda i: (i, 0))
    gspec = lambda k: pl.BlockSpec((tm, d), lambda i: (i, k))
    wspec = pl.BlockSpec((d, d), lambda i: (0, 0))
    out_shape = [jax.ShapeDtypeStruct((m, d), F32)]
    out_specs = [row]
    if final:
        out_shape.append(jax.ShapeDtypeStruct((m, d), F32))
        out_specs.append(row)
    return pl.pallas_call(
        functools.partial(_merge_body, final),
        grid=(m // tm,),
        in_specs=[row, row, row, gspec(0), gspec(1), gspec(2), row, _mod_spec(gate, tm, seq),
                  wspec, wspec, wspec, wspec, pl.BlockSpec((1, d), lambda i: (0, 0))],
        out_specs=out_specs,
        out_shape=out_shape,
        compiler_params=_cparams(("parallel",)),
        name="merge_out",
    )(ya, yb, yc, g, g, g, x2, gate, wa, wb, wc, wo, g_final.reshape(1, d))


def _col_offsets(d):
    sizes = (H_A * DK_A, H_A * DK_A, W_A, W_A,
             H_B * DK_B, H_B * DK_B, W_B, W_B,
             W_C, W_C, W_C, H_C, W_C,
             N_BRANCH * d)
    offs = np.concatenate([[0], np.cumsum(sizes)])
    return [(int(offs[i]), int(offs[i + 1])) for i in range(len(sizes))]


def _rope_tables(pos):
    half = DK_B // 2
    inv = ROPE_BASE ** (-jnp.arange(half, dtype=F32) / half)
    ang = pos.astype(F32)[:, None] * inv[None, :]
    cos = jnp.cos(ang)
    sin = jnp.sin(ang)
    return jnp.concatenate([cos, cos], axis=-1), jnp.concatenate([-sin, sin], axis=-1)


def _pad_tokens(a, batch, seq, seq_pad):
    n = a.shape[-1]
    return jnp.pad(a.reshape(batch, seq, n), ((0, 0), (0, seq_pad - seq), (0, 0))).reshape(batch * seq_pad, n)


def _unpad_tokens(a, batch, seq, seq_pad):
    n = a.shape[-1]
    return a.reshape(batch, seq_pad, n)[:, :seq].reshape(batch * seq, n)


def _layer(layer, x2, batch, seq, mod_l, pos, w, s_hgrn, s_ret, past, final):
    m, d = x2.shape
    tm = min(512, m)
    shift, scale, gate = _mod_arrays(mod_l, batch, seq, tm, d)
    u = _norm_mod(x2, w["g_norm"], shift, scale, tm, seq)

    cols = _col_offsets(d)
    w_in = w["w_in"]
    wcol = lambda i: w_in[:, cols[i][0]:cols[i][1]].astype(BF16)

    tp = min(PROJ_ROWS, m)
    logf_a, k_a, q_a, v_a, z_a = _proj_hgrn(u, wcol(0), wcol(1), wcol(2), wcol(3), w["lb_logits"], layer, tp)
    cos, sin = _rope_tables(pos)
    if cos.shape[0] % tp != 0:
        cos = jnp.tile(cos, (m // cos.shape[0], 1))
        sin = jnp.tile(sin, (m // sin.shape[0], 1))
    q_b, k_b, v_b, z_b = _proj_ret(u, wcol(4), wcol(5), wcol(6), wcol(7), cos, sin, tp)
    q_c, k_c32, k_c16, v_c32, v_c16, z_c = _proj_fox(u, wcol(8), wcol(9), wcol(10), wcol(12), tp)
    logf_c = _proj_foxf(u, w_in[:, cols[11][0]:cols[11][1]], w["b_fox_f"], tp)
    gates = _proj_gate(u, wcol(13), tp)

    if past is None:
        y_a, s_hgrn_new = _hgrn_scan(logf_a, k_a, q_a, v_a, z_a, w["g_hgrn"], s_hgrn, batch, seq)
        y_b, s_ret_new = _ret_scan(q_b, k_b, v_b, z_b, w["g_ret"], s_ret, batch, seq,
                                   min(RET_CHUNK, seq), min(RET_CHUNK, seq))
        f_row = _fcum(logf_c.reshape(batch, seq, H_C).transpose(0, 2, 1)).reshape(batch, H_C, 1, seq)
        y_c = _fox_prompt(q_c, k_c16, v_c16, f_row, z_c, batch, seq)
    else:
        cache_k, cache_v, lf_t, page_table = past
        sp = HGRN_CHUNK
        pad = lambda a: _pad_tokens(a, batch, seq, sp)
        y_a, s_hgrn_new = _hgrn_scan(pad(logf_a), pad(k_a), pad(q_a), pad(v_a), pad(z_a),
                                     w["g_hgrn"], s_hgrn, batch, sp)
        y_a = _unpad_tokens(y_a, batch, seq, sp)
        y_b, s_ret_new = _ret_scan(pad(q_b), pad(k_b), pad(v_b), pad(z_b), w["g_ret"], s_ret,
                                   batch, sp, sp, seq)
        y_b = _unpad_tokens(y_b, batch, seq, sp)

        ps = cache_k.shape[2]
        new_pad = LANES // H_C
        hn = _new_bias(_pad_tokens(logf_c, batch, seq, ps).reshape(batch, ps, H_C))
        hn_rows = hn[:, :new_pad].reshape(batch, 1, new_pad * H_C)
        to_rows = lambda a: a.reshape(batch, seq * H_C, DH_C)
        new_rows = lambda a: _pad_tokens(a, batch, seq, new_pad).reshape(batch, new_pad * H_C, DH_C)
        y_rows = _fox_decode(to_rows(q_c), cache_k, cache_v, lf_t, new_rows(k_c16), new_rows(v_c16),
                             hn_rows, to_rows(z_c), page_table, layer, seq, new_pad)
        y_c = y_rows.reshape(m, W_C)

    bf = lambda name: w[name].astype(BF16)
    outs = _merge(y_a, y_b, y_c, gates, x2, gate, bf("w_proj_a"), bf("w_proj_b"), bf("w_proj_c"),
                  bf("w_out"), w["g_final"], min(256, m), seq, final)
    return outs, s_hgrn_new, s_ret_new, k_c32, v_c32, logf_c


def kernel(x_prompt, x_sample, c_prompt, c_sample, cache_k, cache_v, cache_logf, state_hgrn, state_ret,
           page_table, w_ada, b_ada, g_norm, w_in, b_fox_f, lb_logits, g_hgrn, g_ret, w_proj_a, w_proj_b,
           w_proj_c, w_out, g_final):
    bp, lp, d = x_prompt.shape
    bs, ls, _ = x_sample.shape
    depth = w_in.shape[0]
    n_pages = page_table.shape[1]
    past_len = n_pages * cache_k.shape[2]
    pos_p = jnp.arange(lp, dtype=jnp.int32)
    pos_s = past_len + jnp.arange(ls, dtype=jnp.int32)

    n_c = bp + bs
    rows = -(-n_c // 8) * 8
    c_all = jnp.pad(jnp.concatenate([c_prompt, c_sample], axis=0), ((0, rows - n_c), (0, 0)))
    mod = _ada_mod(c_all, w_ada, b_ada)

    lf_t = jnp.transpose(cache_logf, (1, 0, 3, 2))
    zero_hgrn = jnp.zeros((bp, H_A, DK_A, DV_A), F32)
    zero_ret = jnp.zeros((bp, H_B, DK_B, DV_B), F32)
    xp = x_prompt.reshape(bp * lp, d)
    xs = x_sample.reshape(bs * ls, d)
    acc = {k: [] for k in ("kp", "vp", "fp", "hp", "rp", "ks", "vs", "fs", "hs", "rs")}
    yp = ys = None
    for l in range(depth):
        w = dict(g_norm=g_norm[l], w_in=w_in[l], b_fox_f=b_fox_f[l], lb_logits=lb_logits,
                 g_hgrn=g_hgrn[l], g_ret=g_ret[l], w_proj_a=w_proj_a[l], w_proj_b=w_proj_b[l],
                 w_proj_c=w_proj_c[l], w_out=w_out[l], g_final=g_final)
        final = l == depth - 1
        outs, h, r, k, v, f = _layer(l, xp, bp, lp, mod[l, :bp], pos_p, w, zero_hgrn, zero_ret, None, final)
        xp = outs[0]
        if final:
            yp = outs[1]
        acc["kp"].append(k.reshape(bp, lp, H_C, DH_C)); acc["vp"].append(v.reshape(bp, lp, H_C, DH_C))
        acc["fp"].append(f.reshape(bp, lp, H_C)); acc["hp"].append(h); acc["rp"].append(r)

        outs, h, r, k, v, f = _layer(l, xs, bs, ls, mod[l, bp:bp + bs], pos_s, w, state_hgrn[:, l],
                                     state_ret[:, l], (cache_k, cache_v, lf_t, page_table), final)
        xs = outs[0]
        if final:
            ys = outs[1]
        acc["ks"].append(k.reshape(bs, ls, H_C, DH_C)); acc["vs"].append(v.reshape(bs, ls, H_C, DH_C))
        acc["fs"].append(f.reshape(bs, ls, H_C)); acc["hs"].append(h); acc["rs"].append(r)

    st = lambda name: jnp.stack(acc[name], axis=1)
    return (yp.reshape(bp, lp, d), ys.reshape(bs, ls, d),
            st("kp"), st("vp"), st("fp"), st("hp"), st("rp"),
            st("ks"), st("vs"), st("fs"), st("hs"), st("rs"))
```

```python
import functools
import math

import numpy as np
import jax
import jax.numpy as jnp
from jax import lax
from jax.experimental import pallas as pl
from jax.experimental.pallas import tpu as pltpu

F32 = jnp.float32
BF16 = jnp.bfloat16

H_A, DK_A, DV_A = 8, 128, 128
H_B, DK_B, DV_B = 4, 128, 256
H_C, DH_C = 8, 128
W_A = H_A * DV_A
W_B = H_B * DV_B
W_C = H_C * DH_C
N_BRANCH = 3
ROPE_BASE = 10000.0
RET_DECAY_BASE = 5.0
NORM_EPS = 1e-6
MASK_VALUE = -1e30
LOG2E = math.log2(math.e)

LANES = 128
HGRN_CHUNK = 128
HGRN_HEADS_PER_STEP = 8
PROJ_ROWS = 2048
DECODE_PAGES_PER_GROUP = 8
DECODE_SLOTS = 3
RET_CHUNK = 256
FOX_BLOCK = 512
FOX_SOFTMAX_ROWS = 128
VMEM_LIMIT = 56 * 1024 * 1024


def _cparams(sem):
    return pltpu.CompilerParams(dimension_semantics=sem, vmem_limit_bytes=VMEM_LIMIT)


def _dot(a, b):
    return jnp.dot(a, b, preferred_element_type=F32)


def _dot_nt(a, b):
    return lax.dot_general(a, b, (((1,), (1,)), ((), ())), preferred_element_type=F32)


def _split3(x):
    hi = x.astype(BF16)
    r = x - hi.astype(F32)
    mid = r.astype(BF16)
    lo = (r - mid.astype(F32)).astype(BF16)
    return hi, mid, lo


def _dot_exact_lhs(t, x, terms=3):
    parts = _split3(x)[:terms]
    out = _dot(t, parts[0])
    for part in parts[1:]:
        out = out + _dot(t, part)
    return out


def _dot_exact_rhs(x, t):
    hi, mid, lo = _split3(x)
    return _dot(hi, t) + _dot(mid, t) + _dot(lo, t)


def _sigmoid(x):
    return 1.0 / (1.0 + jnp.exp(-x))


def _silu(x):
    return x * _sigmoid(x)


def _log_sigmoid(x):
    return jnp.minimum(x, 0.0) - jnp.log(1.0 + jnp.exp(-jnp.abs(x)))


def _ada_body(c_ref, w_ref, b_ref, o_ref):
    c = c_ref[...]
    w = w_ref[0]
    c_hi = c.astype(BF16)
    c_lo = (c - c_hi.astype(F32)).astype(BF16)
    w_hi = w.astype(BF16)
    w_lo = (w - w_hi.astype(F32)).astype(BF16)
    o_ref[0] = _dot(c_hi, w_hi) + _dot(c_hi, w_lo) + _dot(c_lo, w_hi) + b_ref[0]


def _ada_mod(c_all, w_ada, b_ada):
    depth, d, d3 = w_ada.shape
    rows = c_all.shape[0]
    return pl.pallas_call(
        _ada_body,
        grid=(depth, d3 // d),
        in_specs=[pl.BlockSpec((rows, d), lambda l, j: (0, 0)),
                  pl.BlockSpec((1, d, d), lambda l, j: (l, 0, j)),
                  pl.BlockSpec((1, 1, d), lambda l, j: (l, 0, j))],
        out_specs=pl.BlockSpec((1, rows, d), lambda l, j: (l, 0, j)),
        out_shape=jax.ShapeDtypeStruct((depth, rows, d3), F32),
        compiler_params=_cparams(("arbitrary", "arbitrary")),
        name="ada_mod",
    )(c_all, w_ada, b_ada.reshape(depth, 1, d3))


def _mod_spec(mod, tm, seq):
    if mod.ndim == 3:
        return pl.BlockSpec((None, 1, mod.shape[-1]), lambda i: ((i * tm) // seq, 0, 0))
    return pl.BlockSpec((tm, mod.shape[-1]), lambda i: (i, 0))


def _mod_arrays(mod_l, batch, seq, tm, d):
    parts = [mod_l[:, k * d:(k + 1) * d] for k in range(3)]
    if seq % tm == 0:
        return [p.reshape(batch, 1, d) for p in parts]
    return [jnp.broadcast_to(p[:, None, :], (batch, seq, d)).reshape(batch * seq, d) for p in parts]


def _norm_body(x_ref, g_ref, sh_ref, sc_ref, u_ref):
    x = x_ref[...]
    ms = jnp.mean(x * x, axis=-1, keepdims=True)
    y = x * lax.rsqrt(ms + NORM_EPS) * g_ref[...]
    u_ref[...] = (y * (1.0 + sc_ref[...]) + sh_ref[...]).astype(BF16)


def _norm_mod(x2, g, shift, scale, tm, seq):
    m, d = x2.shape
    return pl.pallas_call(
        _norm_body,
        grid=(m // tm,),
        in_specs=[pl.BlockSpec((tm, d), lambda i: (i, 0)),
                  pl.BlockSpec((1, d), lambda i: (0, 0)),
                  _mod_spec(shift, tm, seq), _mod_spec(scale, tm, seq)],
        out_specs=pl.BlockSpec((tm, d), lambda i: (i, 0)),
        out_shape=jax.ShapeDtypeStruct((m, d), BF16),
        compiler_params=_cparams(("parallel",)),
        name="norm_mod",
    )(x2, g.reshape(1, d), shift, scale)


def _proj_hgrn_body(layer, u_ref, wf_ref, wq_ref, wi_ref, wz_ref, lbl_ref,
                    logf_ref, k_ref, q_ref, v_ref, z_ref):
    u = u_ref[...]
    lbl = lbl_ref[...]
    e = jnp.exp(lbl - jnp.max(lbl, axis=0, keepdims=True))
    sm = e / jnp.sum(e, axis=0, keepdims=True)
    lb = jnp.zeros_like(sm[0:1])
    for r in range(1, layer + 1):
        lb = lb + sm[r:r + 1]
    fl = _dot(u, wf_ref[...])
    logf_ref[...] = jnp.log(lb + (1.0 - lb) * _sigmoid(fl))
    k_ref[...] = (1.0 - lb) * _sigmoid(-fl)
    q_ref[...] = _silu(_dot(u, wq_ref[...])) * (DK_A ** -0.5)
    v_ref[...] = _dot(u, wi_ref[...]).astype(BF16)
    z_ref[...] = _silu(_dot(u, wz_ref[...])).astype(BF16)


def _proj_hgrn(u, wf, wq, wi, wz, lb_logits, layer, tm, tn=256):
    m, d = u.shape
    n = wf.shape[1]
    depth = lb_logits.shape[0]
    wspec = pl.BlockSpec((d, tn), lambda j, i: (0, j))
    ospec = pl.BlockSpec((tm, tn), lambda j, i: (i, j))
    return pl.pallas_call(
        functools.partial(_proj_hgrn_body, layer),
        grid=(n // tn, m // tm),
        in_specs=[pl.BlockSpec((tm, d), lambda j, i: (i, 0)), wspec, wspec, wspec, wspec,
                  pl.BlockSpec((depth, tn), lambda j, i: (0, j))],
        out_specs=[ospec] * 5,
        out_shape=[jax.ShapeDtypeStruct((m, n), F32), jax.ShapeDtypeStruct((m, n), F32),
                   jax.ShapeDtypeStruct((m, n), F32), jax.ShapeDtypeStruct((m, n), BF16),
                   jax.ShapeDtypeStruct((m, n), BF16)],
        compiler_params=_cparams(("parallel", "parallel")),
        name="proj_hgrn",
    )(u, wf, wq, wi, wz, lb_logits)


def _rope(x, cos, sin_signed):
    return x * cos + pltpu.roll(x, DK_B // 2, axis=1) * sin_signed


def _proj_ret_body(u_ref, wq_ref, wk_ref, wv_ref, wz_ref, cos_ref, sin_ref,
                   q_ref, k_ref, v_ref, z_ref):
    u = u_ref[...]
    cos = cos_ref[...]
    sin = sin_ref[...]
    q_ref[...] = _rope(_dot(u, wq_ref[...]), cos, sin) * (DK_B ** -0.5)
    k_ref[...] = _rope(_dot(u, wk_ref[...]), cos, sin)
    v_ref[...] = _dot(u, wv_ref[...]).astype(BF16)
    z_ref[...] = _silu(_dot(u, wz_ref[...])).astype(BF16)


def _proj_ret(u, wq, wk, wv, wz, cos, sin, tm):
    m, d = u.shape
    n_rope_tiles = cos.shape[0] // tm
    qk_w = pl.BlockSpec((d, DK_B), lambda h, i: (0, h))
    vz_w = pl.BlockSpec((d, DV_B), lambda h, i: (0, h))
    rope_spec = pl.BlockSpec((tm, DK_B), lambda h, i: (i % n_rope_tiles, 0))
    qk_o = pl.BlockSpec((tm, DK_B), lambda h, i: (i, h))
    vz_o = pl.BlockSpec((tm, DV_B), lambda h, i: (i, h))
    return pl.pallas_call(
        _proj_ret_body,
        grid=(H_B, m // tm),
        in_specs=[pl.BlockSpec((tm, d), lambda h, i: (i, 0)), qk_w, qk_w, vz_w, vz_w,
                  rope_spec, rope_spec],
        out_specs=[qk_o, qk_o, vz_o, vz_o],
        out_shape=[jax.ShapeDtypeStruct((m, H_B * DK_B), F32), jax.ShapeDtypeStruct((m, H_B * DK_B), F32),
                   jax.ShapeDtypeStruct((m, W_B), BF16), jax.ShapeDtypeStruct((m, W_B), BF16)],
        compiler_params=_cparams(("parallel", "parallel")),
        name="proj_ret",
    )(u, wq, wk, wv, wz, cos, sin)


def _proj_fox_body(u_ref, wq_ref, wk_ref, wv_ref, wz_ref,
                   q16_ref, k32_ref, k16_ref, v32_ref, v16_ref, z_ref):
    u = u_ref[...]
    q16_ref[...] = _dot(u, wq_ref[...]).astype(BF16)
    k = _dot(u, wk_ref[...])
    k32_ref[...] = k
    k16_ref[...] = k.astype(BF16)
    v = _dot(u, wv_ref[...])
    v32_ref[...] = v
    v16_ref[...] = v.astype(BF16)
    z_ref[...] = _silu(_dot(u, wz_ref[...])).astype(BF16)


def _proj_fox(u, wq, wk, wv, wz, tm, tn=256):
    m, d = u.shape
    n = wq.shape[1]
    wspec = pl.BlockSpec((d, tn), lambda j, i: (0, j))
    ospec = pl.BlockSpec((tm, tn), lambda j, i: (i, j))
    shp = lambda dt: jax.ShapeDtypeStruct((m, n), dt)
    return pl.pallas_call(
        _proj_fox_body,
        grid=(n // tn, m // tm),
        in_specs=[pl.BlockSpec((tm, d), lambda j, i: (i, 0)), wspec, wspec, wspec, wspec],
        out_specs=[ospec] * 6,
        out_shape=[shp(BF16), shp(F32), shp(BF16), shp(F32), shp(BF16), shp(BF16)],
        compiler_params=_cparams(("parallel", "parallel")),
        name="proj_fox",
    )(u, wq, wk, wv, wz)


def _proj_foxf_body(u_ref, whi_ref, wlo_ref, b_ref, o_ref):
    u = u_ref[...]
    cf = _dot(u, whi_ref[...]) + _dot(u, wlo_ref[...]) + b_ref[...]
    o_ref[...] = _log_sigmoid(cf)[:, :H_C]


def _proj_foxf(u, wf, bf, tm):
    m, d = u.shape
    wpad = jnp.pad(wf, ((0, 0), (0, LANES - H_C)))
    whi = wpad.astype(BF16)
    wlo = (wpad - whi.astype(F32)).astype(BF16)
    bpad = jnp.pad(bf.reshape(1, H_C), ((0, 0), (0, LANES - H_C)))
    wspec = pl.BlockSpec((d, LANES), lambda i: (0, 0))
    return pl.pallas_call(
        _proj_foxf_body,
        grid=(m // tm,),
        in_specs=[pl.BlockSpec((tm, d), lambda i: (i, 0)), wspec, wspec,
                  pl.BlockSpec((1, LANES), lambda i: (0, 0))],
        out_specs=pl.BlockSpec((tm, H_C), lambda i: (i, 0)),
        out_shape=jax.ShapeDtypeStruct((m, H_C), F32),
        compiler_params=_cparams(("parallel",)),
        name="proj_foxf",
    )(u, whi, wlo, bpad)


def _proj_gate_body(u_ref, w_ref, o_ref):
    o_ref[...] = _sigmoid(_dot(u_ref[...], w_ref[...])).astype(BF16)


def _proj_gate(u, wg, tm, tn=512):
    m, d = u.shape
    n = wg.shape[1]
    return pl.pallas_call(
        _proj_gate_body,
        grid=(n // tn, m // tm),
        in_specs=[pl.BlockSpec((tm, d), lambda j, i: (i, 0)),
                  pl.BlockSpec((d, tn), lambda j, i: (0, j))],
        out_specs=pl.BlockSpec((tm, tn), lambda j, i: (i, j)),
        out_shape=jax.ShapeDtypeStruct((m, n), BF16),
        compiler_params=_cparams(("parallel", "parallel")),
        name="proj_gate",
    )(u, wg)


def _hgrn_levels(c):
    return [c >> (i + 1) for i in range(int(math.log2(c)))]


def _hgrn_tables(c):
    t = np.arange(c)[:, None]
    u = np.arange(c)[None, :]
    mats, masks = [], []
    for w in _hgrn_levels(c):
        r = (t // (2 * w)) * (2 * w) + w
        upper = (t % (2 * w)) >= w
        mats.append(np.where(upper, (u >= r) & (u <= t), (u > t) & (u < r)))
        masks.append((t // (2 * w)) == (u // (2 * w)))
    mats.append(u <= t)
    mats.append(u > t)
    masks.append(t == u)
    return (jnp.asarray(np.concatenate(mats, 0).astype(np.float32), BF16),
            jnp.asarray(np.stack(masks, 0).astype(np.float32), F32))


def _hgrn_body(c, nc, hp, logf_ref, k_ref, q_ref, v_ref, z_ref, g_ref, s0_ref, t_ref, mask_ref,
               y_ref, sout_ref, st_ref):
    ci = pl.program_id(2)

    @pl.when(ci == 0)
    def _():
        for hh in range(hp):
            st_ref[hh] = s0_ref[0, hh].T

    levels = _hgrn_levels(c)
    nl = len(levels)
    e_all = jnp.exp(_dot_exact_lhs(t_ref[...], logf_ref[...], terms=2))
    row = lax.broadcasted_iota(jnp.int32, (c, DK_A), 0)
    g = g_ref[...]
    for hh in range(hp):
        cols = slice(hh * DK_A, (hh + 1) * DK_A)
        q = q_ref[:, cols]
        k = k_ref[:, cols]
        v = v_ref[:, cols]
        a = mask_ref[nl] * _dot_nt(q.astype(BF16), k.astype(BF16))
        for li, w in enumerate(levels):
            e_w = e_all[li * c:(li + 1) * c, cols]
            upper = (row & w) != 0
            q_w = jnp.where(upper, q * e_w, 0.0).astype(BF16)
            k_w = jnp.where(upper, 0.0, k * e_w).astype(BF16)
            a = a + mask_ref[li] * _dot_nt(q_w, k_w)
        e_b = e_all[nl * c:(nl + 1) * c, cols]
        e_end = e_all[(nl + 1) * c:(nl + 2) * c, cols]
        st = st_ref[hh]
        o = _dot(a.astype(BF16), v) + _dot_nt((q * e_b).astype(BF16), st.astype(BF16))
        v_t = v.astype(F32).T.astype(BF16)
        st_ref[hh] = st * e_b[c - 1:c, :] + _dot(v_t, (k * e_end).astype(BF16))
        ms = jnp.mean(o * o, axis=-1, keepdims=True)
        y_ref[:, cols] = (o * lax.rsqrt(ms + NORM_EPS) * g * z_ref[:, cols]).astype(BF16)

    @pl.when(ci == nc - 1)
    def _():
        for hh in range(hp):
            sout_ref[0, hh] = st_ref[hh].T


def _hgrn_scan(logf, k, q, v, z, g, s0, batch, seq):
    c = HGRN_CHUNK
    hp = HGRN_HEADS_PER_STEP
    nc = seq // c
    m = batch * seq
    tmat, masks = _hgrn_tables(c)
    tok = pl.BlockSpec((c, hp * DK_A), lambda b, h, ci: (b * nc + ci, h))
    st_spec = pl.BlockSpec((1, hp, DK_A, DV_A), lambda b, h, ci: (b, h, 0, 0))
    return pl.pallas_call(
        functools.partial(_hgrn_body, c, nc, hp),
        grid=(batch, H_A // hp, nc),
        in_specs=[tok, tok, tok, tok, tok,
                  pl.BlockSpec((1, DV_A), lambda b, h, ci: (0, 0)),
                  st_spec,
                  pl.BlockSpec(tmat.shape, lambda b, h, ci: (0, 0)),
                  pl.BlockSpec(masks.shape, lambda b, h, ci: (0, 0, 0))],
        out_specs=[tok, st_spec],
        out_shape=[jax.ShapeDtypeStruct((m, W_A), BF16),
                   jax.ShapeDtypeStruct((batch, H_A, DK_A, DV_A), F32)],
        scratch_shapes=[pltpu.VMEM((hp, DV_A, DK_A), F32)],
        compiler_params=_cparams(("parallel", "parallel", "arbitrary")),
        name="hgrn_scan",
    )(logf, k, q, v, z, g.reshape(1, DV_A), s0, tmat, masks)


def _ret_body(c, c_eff, nc, lg_ref, q_ref, k_ref, v_ref, z_ref, g_ref, s0_ref,
              y_ref, sout_ref, s_ref):
    ci = pl.program_id(1)

    @pl.when(ci == 0)
    def _():
        s_ref[...] = s0_ref[0]

    t = lax.broadcasted_iota(jnp.int32, (c, DK_B), 0).astype(F32)
    rel = (lax.broadcasted_iota(jnp.int32, (c, c), 0)
           - lax.broadcasted_iota(jnp.int32, (c, c), 1)).astype(F32)
    g = g_ref[...]
    for h in range(H_B):
        qk_cols = slice(h * DK_B, (h + 1) * DK_B)
        vz_cols = slice(h * DV_B, (h + 1) * DV_B)
        lg = lg_ref[h]
        q = q_ref[:, qk_cols]
        k = k_ref[:, qk_cols]
        v = v_ref[:, vz_cols]
        q_dec = jnp.exp((t + 1.0) * lg)
        k_dec = jnp.where(t < c_eff, jnp.exp(jnp.maximum(c_eff - 1.0 - t, 0.0) * lg), 0.0)
        intra = jnp.where(rel >= 0.0, jnp.exp(jnp.maximum(rel, 0.0) * lg), 0.0)
        a = _dot_nt(q.astype(BF16), k.astype(BF16)) * intra
        s = s_ref[h]
        o = _dot(a.astype(BF16), v) + _dot((q * q_dec).astype(BF16), s.astype(BF16))
        kd_t = (k * k_dec).T.astype(BF16)
        chunk_dec = jnp.exp(jnp.full((1, DV_B), c_eff, F32) * lg)
        s_ref[h] = chunk_dec * s + _dot(kd_t, v)
        ms = jnp.mean(o * o, axis=-1, keepdims=True)
        y_ref[:, vz_cols] = (o * lax.rsqrt(ms + NORM_EPS) * g * z_ref[:, vz_cols]).astype(BF16)

    @pl.when(ci == nc - 1)
    def _():
        sout_ref[0] = s_ref[...]


def _ret_scan(q, k, v, z, g, s0, batch, seq, c, c_eff):
    nc = seq // c
    m = batch * seq
    log_gamma = jnp.log1p(-jnp.power(2.0, -RET_DECAY_BASE - jnp.arange(H_B, dtype=F32)))
    qk = pl.BlockSpec((c, H_B * DK_B), lambda b, ci: (b * nc + ci, 0))
    vz = pl.BlockSpec((c, W_B), lambda b, ci: (b * nc + ci, 0))
    st_spec = pl.BlockSpec((1, H_B, DK_B, DV_B), lambda b, ci: (b, 0, 0, 0))
    return pl.pallas_call(
        functools.partial(_ret_body, c, c_eff, nc),
        grid=(batch, nc),
        in_specs=[pl.BlockSpec(memory_space=pltpu.SMEM), qk, qk, vz, vz,
                  pl.BlockSpec((1, DV_B), lambda b, ci: (0, 0)), st_spec],
        out_specs=[vz, st_spec],
        out_shape=[jax.ShapeDtypeStruct((m, W_B), BF16),
                   jax.ShapeDtypeStruct((batch, H_B, DK_B, DV_B), F32)],
        scratch_shapes=[pltpu.VMEM((H_B, DK_B, DV_B), F32)],
        compiler_params=_cparams(("parallel", "arbitrary")),
        name="ret_scan",
    )(log_gamma, q, k, v, z, g.reshape(1, DV_B), s0)


def _fcum_body(nblk, x_ref, o_ref):
    r = lax.broadcasted_iota(jnp.int32, (LANES, LANES), 0)
    cc = lax.broadcasted_iota(jnp.int32, (LANES, LANES), 1)
    tri = jnp.where(r <= cc, 1.0, 0.0).astype(BF16)
    carry = jnp.zeros((H_C, 1), F32)
    for blk in range(nblk):
        x = x_ref[0, :, blk * LANES:(blk + 1) * LANES]
        y = _dot_exact_rhs(x, tri) + carry
        o_ref[0, :, blk * LANES:(blk + 1) * LANES] = y * LOG2E
        carry = y[:, LANES - 1:LANES]


def _fcum(logf_t):
    b, h, seq = logf_t.shape
    spec = pl.BlockSpec((1, h, seq), lambda i: (i, 0, 0))
    return pl.pallas_call(
        functools.partial(_fcum_body, seq // LANES),
        grid=(b,),
        in_specs=[spec], out_specs=spec,
        out_shape=jax.ShapeDtypeStruct((b, h, seq), F32),
        compiler_params=_cparams(("parallel",)),
        name="fox_fcum",
    )(logf_t)


def _fold_lanes(x, op):
    out = x[:, 0:LANES]
    for c in range(1, x.shape[1] // LANES):
        out = op(out, x[:, c * LANES:(c + 1) * LANES])
    return out


def _fox_prompt_body(tq, q_ref, k_ref, v_ref, fq_ref, fk_ref, z_ref, o_ref, fqc_ref, t_ref, p_ref, st_ref):
    i = pl.program_id(2)
    q = (q_ref[...].astype(F32) * (DH_C ** -0.5 * LOG2E)).astype(BF16)
    fqc_ref[...] = jnp.broadcast_to(fq_ref[0, 0], (LANES, tq)).T
    m_ref, l_ref, a_ref, acc_ref = st_ref.at[0], st_ref.at[1], st_ref.at[2], st_ref.at[3]
    m_ref[...] = jnp.full((tq, LANES), MASK_VALUE, F32)
    l_ref[...] = jnp.zeros((tq, LANES), F32)
    a_ref[...] = jnp.ones((tq, LANES), F32)
    acc_ref[...] = jnp.zeros((tq, DH_C), F32)

    def scores(j):
        start = pl.multiple_of(j * tq, tq)
        return _dot_nt(q, k_ref[pl.ds(start, tq), :]) - fk_ref[0, 0, :, pl.ds(start, tq)]

    def softmax(slot, diagonal):
        for r in range(tq // FOX_SOFTMAX_ROWS):
            rows = slice(r * FOX_SOFTMAX_ROWS, (r + 1) * FOX_SOFTMAX_ROWS)
            t = t_ref[slot, rows, :]
            if diagonal:
                rr = lax.broadcasted_iota(jnp.int32, t.shape, 0) + r * FOX_SOFTMAX_ROWS
                cc = lax.broadcasted_iota(jnp.int32, t.shape, 1)
                t = jnp.where(rr >= cc, t, MASK_VALUE)
            fq = fqc_ref[rows, :]
            m_prev = m_ref[rows, :]
            m_new = jnp.maximum(m_prev, jnp.max(_fold_lanes(t, jnp.maximum), axis=-1, keepdims=True) + fq)
            p = jnp.exp2(t + jnp.concatenate([fq - m_new] * (tq // LANES), axis=1))
            alpha = jnp.exp2(m_prev - m_new)
            l_ref[rows, :] = alpha * l_ref[rows, :] + jnp.sum(_fold_lanes(p, jnp.add), axis=-1, keepdims=True)
            a_ref[rows, :] = alpha
            m_ref[rows, :] = m_new
            p_ref[slot, rows, :] = p.astype(BF16)

    def weighted(j, slot):
        start = pl.multiple_of(j * tq, tq)
        acc_ref[...] = a_ref[...] * acc_ref[...] + _dot(p_ref[slot], v_ref[pl.ds(start, tq), :])

    t_ref[0] = scores(0)
    p_ref[1] = jnp.zeros((tq, tq), BF16)

    def trip(j, carry):
        slot = j % 2
        weighted(jnp.maximum(j - 1, 0), 1 - slot)
        softmax(slot, False)
        t_ref[1 - slot] = scores(j + 1)
        return carry

    lax.fori_loop(0, i, trip, 0)
    slot = i % 2
    weighted(jnp.maximum(i - 1, 0), 1 - slot)
    softmax(slot, True)
    weighted(i, slot)
    o_ref[...] = (acc_ref[...] / l_ref[...] * z_ref[...]).astype(BF16)


def _fox_prompt(q16, k16, v16, f_row, z, batch, seq):
    tq = min(FOX_BLOCK, seq)
    nq = seq // tq
    m = batch * seq
    qspec = pl.BlockSpec((tq, DH_C), lambda b, h, i: (b * nq + i, h))
    kvspec = pl.BlockSpec((seq, DH_C), lambda b, h, i: (b, h))
    return pl.pallas_call(
        functools.partial(_fox_prompt_body, tq),
        grid=(batch, H_C, nq),
        in_specs=[qspec, kvspec, kvspec,
                  pl.BlockSpec((1, 1, 1, tq), lambda b, h, i: (b, h, 0, i)),
                  pl.BlockSpec((1, 1, 1, seq), lambda b, h, i: (b, h, 0, 0)),
                  qspec],
        out_specs=qspec,
        out_shape=jax.ShapeDtypeStruct((m, W_C), BF16),
        scratch_shapes=[pltpu.VMEM((tq, LANES), F32), pltpu.VMEM((2, tq, tq), F32),
                        pltpu.VMEM((2, tq, tq), BF16), pltpu.VMEM((4, tq, LANES), F32)],
        compiler_params=_cparams(("parallel", "parallel", "arbitrary")),
        name="fox_prompt",
    )(q16, k16, v16, f_row, f_row, z)


def _tri_f32(n, kind):
    r = lax.broadcasted_iota(jnp.int32, (n, n), 0)
    c = lax.broadcasted_iota(jnp.int32, (n, n), 1)
    cond = (c > r) if kind == "after" else (c <= r)
    return jnp.where(cond, 1.0, 0.0).astype(BF16)


def _new_bias_body(n, x_ref, o_ref):
    o_ref[0] = -_dot_exact_lhs(_tri_f32(n, "upto"), x_ref[0])


def _new_bias(logf_new_pad):
    nb, n, h = logf_new_pad.shape
    spec = pl.BlockSpec((1, n, h), lambda b: (b, 0, 0))
    return pl.pallas_call(
        functools.partial(_new_bias_body, n),
        grid=(nb,),
        in_specs=[spec], out_specs=spec,
        out_shape=jax.ShapeDtypeStruct((nb, n, h), F32),
        compiler_params=_cparams(("parallel",)),
        name="fox_new_bias",
    )(logf_new_pad)


def _fox_decode_body(n_groups, pg, ps, n_new, new_pad, layer, n_pages, pt_ref, q_ref, k_hbm, v_hbm, lf_hbm,
                     rep_ref, kn_ref, vn_ref, hn_ref, z_ref, o_ref, kbuf, vbuf, lfbuf, sem):
    b = pl.program_id(0)
    scale = DH_C ** -0.5
    rows = H_C * n_new
    cols = ps * H_C
    n_slots = kbuf.shape[0]

    def group_copies(g, slot):
        out = []
        for i in range(pg):
            page = pt_ref[b, n_pages - 1 - (g * pg + i)]
            out.append(pltpu.make_async_copy(k_hbm.at[page, layer], kbuf.at[slot, i], sem.at[0, slot]))
            out.append(pltpu.make_async_copy(v_hbm.at[page, layer], vbuf.at[slot, i], sem.at[1, slot]))
            out.append(pltpu.make_async_copy(lf_hbm.at[layer, page], lfbuf.at[slot, i], sem.at[2, slot]))
        return out

    for g0 in range(min(n_slots - 1, n_groups)):
        for cp in group_copies(g0, g0):
            cp.start()

    def group(g, state):
        m_prev, l_prev, acc_prev, carry = state
        slot = g % n_slots
        ahead = g + n_slots - 1

        @pl.when(ahead < n_groups)
        def _():
            for cp in group_copies(ahead, ahead % n_slots):
                cp.start()

        for cp in group_copies(g, slot):
            cp.wait()

        lf = jnp.concatenate([lfbuf[slot, i] for i in range(pg)], axis=0)
        t_from = lax.broadcasted_iota(jnp.int32, (ps, ps), 0)
        t_to = lax.broadcasted_iota(jnp.int32, (ps, ps), 1)
        later = jnp.where(t_from > t_to, 1.0, 0.0).astype(BF16)
        within = _dot_exact_rhs(lf, later)
        total = jnp.sum(lf, axis=-1, keepdims=True)
        bias = []
        for i in range(pg):
            bias.append(within[i * H_C:(i + 1) * H_C] + carry)
            carry = carry + total[i * H_C:(i + 1) * H_C]
        spread = _dot_exact_rhs(jnp.concatenate(bias, axis=0), rep_ref[...])
        other_head = jnp.where(lax.broadcasted_iota(jnp.int32, (H_C, cols), 0)
                               == lax.broadcasted_iota(jnp.int32, (H_C, cols), 1) % H_C, 0.0, MASK_VALUE)

        q = q_ref[0]
        s_list = []
        for i in range(pg):
            kf = kbuf[slot, i].reshape(cols, DH_C).astype(BF16)
            b8 = spread[i * H_C:(i + 1) * H_C] + other_head
            s_list.append(_dot_nt(q, kf) * scale + jnp.concatenate([b8] * n_new, axis=0))
        mx = s_list[0]
        for s in s_list[1:]:
            mx = jnp.maximum(mx, s)
        m_new = jnp.maximum(m_prev, jnp.max(_fold_lanes(mx, jnp.maximum), axis=-1, keepdims=True))
        alpha = jnp.exp(m_prev - m_new)
        psum = None
        pv = None
        for i in range(pg):
            p = jnp.exp(s_list[i] - m_new)
            contrib = _dot(p.astype(BF16), vbuf[slot, i].reshape(cols, DH_C).astype(BF16))
            psum = p if psum is None else psum + p
            pv = contrib if pv is None else pv + contrib
        l_new = alpha * l_prev + jnp.sum(_fold_lanes(psum, jnp.add), axis=-1, keepdims=True)
        return m_new, l_new, alpha * acc_prev + pv, carry

    init = (jnp.full((rows, 1), MASK_VALUE, F32), jnp.zeros((rows, 1), F32), jnp.zeros((rows, DH_C), F32),
            jnp.zeros((H_C, 1), F32))
    m_prev, l_prev, acc_prev, _ = lax.fori_loop(0, n_groups, group, init)

    ncol = new_pad * H_C
    s = _dot_nt(q_ref[0], kn_ref[0]) * scale + hn_ref[0]
    rr = lax.broadcasted_iota(jnp.int32, (rows, ncol), 0)
    cc = lax.broadcasted_iota(jnp.int32, (rows, ncol), 1)
    tok = cc // H_C
    ok = ((rr % H_C) == (cc % H_C)) & (tok <= (rr // H_C)) & (tok < n_new)
    s = jnp.where(ok, s, MASK_VALUE)
    m_new = jnp.maximum(m_prev, jnp.max(s, axis=-1, keepdims=True))
    p = jnp.exp(s - m_new)
    alpha = jnp.exp(m_prev - m_new)
    l_fin = alpha * l_prev + jnp.sum(p, axis=-1, keepdims=True)
    acc = alpha * acc_prev + _dot(p.astype(BF16), vn_ref[0])
    o_ref[0] = (acc / l_fin * z_ref[0]).astype(BF16)


def _fox_decode(q_rows, cache_k, cache_v, lf_t, k_new, v_new, hn_rows, z_rows, page_table, layer,
                n_new, new_pad):
    nb, n_pages = page_table.shape
    ps = cache_k.shape[2]
    pg = min(DECODE_PAGES_PER_GROUP, n_pages)
    n_groups = n_pages // pg
    rows = H_C * n_new
    cols = ps * H_C
    ncol = new_pad * H_C
    rep = jnp.asarray((np.arange(ps)[:, None] == np.arange(cols)[None, :] // H_C).astype(np.float32), BF16)

    hbm = pl.BlockSpec(memory_space=pltpu.HBM)
    per_b = lambda shape: pl.BlockSpec((1,) + shape, lambda b, pt: (b, 0, 0))
    grid_spec = pltpu.PrefetchScalarGridSpec(
        num_scalar_prefetch=1,
        grid=(nb,),
        in_specs=[per_b((rows, DH_C)), hbm, hbm, hbm,
                  pl.BlockSpec((ps, cols), lambda b, pt: (0, 0)),
                  per_b((ncol, DH_C)), per_b((ncol, DH_C)), per_b((1, ncol)), per_b((rows, DH_C))],
        out_specs=per_b((rows, DH_C)),
        scratch_shapes=[pltpu.VMEM((DECODE_SLOTS, pg, ps, H_C, DH_C), F32),
                        pltpu.VMEM((DECODE_SLOTS, pg, ps, H_C, DH_C), F32),
                        pltpu.VMEM((DECODE_SLOTS, pg, H_C, ps), F32),
                        pltpu.SemaphoreType.DMA((3, DECODE_SLOTS))],
    )
    return pl.pallas_call(
        functools.partial(_fox_decode_body, n_groups, pg, ps, n_new, new_pad, layer, n_pages),
        grid_spec=grid_spec,
        out_shape=jax.ShapeDtypeStruct((nb, rows, DH_C), BF16),
        compiler_params=_cparams(("arbitrary",)),
        name="fox_decode",
    )(page_table, q_rows, cache_k, cache_v, lf_t, rep, k_new, v_new, hn_rows, z_rows)


def _merge_body(final, ya_ref, yb_ref, yc_ref, g0_ref, g1_ref, g2_ref, x_ref, gate_ref,
                wa_ref, wb_ref, wc_ref, wo_ref, gf_ref, xo_ref, *maybe_y):
    merged = (g0_ref[...] * _dot(ya_ref[...], wa_ref[...])
              + g1_ref[...] * _dot(yb_ref[...], wb_ref[...])
              + g2_ref[...] * _dot(yc_ref[...], wc_ref[...]))
    xn = x_ref[...] + gate_ref[...] * _dot(merged.astype(BF16), wo_ref[...])
    xo_ref[...] = xn
    if final:
        ms = jnp.mean(xn * xn, axis=-1, keepdims=True)
        maybe_y[0][...] = xn * lax.rsqrt(ms + NORM_EPS) * gf_ref[...]


def _merge(ya, yb, yc, g, x2, gate, wa, wb, wc, wo, g_final, tm, seq, final):
    m, d = x2.shape
    row = pl.BlockSpec((tm, d), lambda i: (i, 0))
    gspec = lambda k: pl.BlockSpec((tm, d), lambda i: (i, k))
    wspec = pl.BlockSpec((d, d), lambda i: (0, 0))
    out_shape = [jax.ShapeDtypeStruct((m, d), F32)]
    out_specs = [row]
    if final:
        out_shape.append(jax.ShapeDtypeStruct((m, d), F32))
        out_specs.append(row)
    return pl.pallas_call(
        functools.partial(_merge_body, final),
        grid=(m // tm,),
        in_specs=[row, row, row, gspec(0), gspec(1), gspec(2), row, _mod_spec(gate, tm, seq),
                  wspec, wspec, wspec, wspec, pl.BlockSpec((1, d), lambda i: (0, 0))],
        out_specs=out_specs,
        out_shape=out_shape,
        compiler_params=_cparams(("parallel",)),
        name="merge_out",
    )(ya, yb, yc, g, g, g, x2, gate, wa, wb, wc, wo, g_final.reshape(1, d))


def _col_offsets(d):
    sizes = (H_A * DK_A, H_A * DK_A, W_A, W_A,
             H_B * DK_B, H_B * DK_B, W_B, W_B,
             W_C, W_C, W_C, H_C, W_C,
             N_BRANCH * d)
    offs = np.concatenate([[0], np.cumsum(sizes)])
    return [(int(offs[i]), int(offs[i + 1])) for i in range(len(sizes))]


def _rope_tables(pos):
    half = DK_B // 2
    inv = ROPE_BASE ** (-jnp.arange(half, dtype=F32) / half)
    ang = pos.astype(F32)[:, None] * inv[None, :]
    cos = jnp.cos(ang)
    sin = jnp.sin(ang)
    return jnp.concatenate([cos, cos], axis=-1), jnp.concatenate([-sin, sin], axis=-1)


def _pad_tokens(a, batch, seq, seq_pad):
    n = a.shape[-1]
    return jnp.pad(a.reshape(batch, seq, n), ((0, 0), (0, seq_pad - seq), (0, 0))).reshape(batch * seq_pad, n)


def _unpad_tokens(a, batch, seq, seq_pad):
    n = a.shape[-1]
    return a.reshape(batch, seq_pad, n)[:, :seq].reshape(batch * seq, n)


def _layer(layer, x2, batch, seq, mod_l, pos, w, s_hgrn, s_ret, past, final):
    m, d = x2.shape
    tm = min(512, m)
    shift, scale, gate = _mod_arrays(mod_l, batch, seq, tm, d)
    u = _norm_mod(x2, w["g_norm"], shift, scale, tm, seq)

    cols = _col_offsets(d)
    w_in = w["w_in"]
    wcol = lambda i: w_in[:, cols[i][0]:cols[i][1]].astype(BF16)

    tp = min(PROJ_ROWS, m)
    logf_a, k_a, q_a, v_a, z_a = _proj_hgrn(u, wcol(0), wcol(1), wcol(2), wcol(3), w["lb_logits"], layer, tp)
    cos, sin = _rope_tables(pos)
    if cos.shape[0] % tp != 0:
        cos = jnp.tile(cos, (m // cos.shape[0], 1))
        sin = jnp.tile(sin, (m // sin.shape[0], 1))
    q_b, k_b, v_b, z_b = _proj_ret(u, wcol(4), wcol(5), wcol(6), wcol(7), cos, sin, tp)
    q_c, k_c32, k_c16, v_c32, v_c16, z_c = _proj_fox(u, wcol(8), wcol(9), wcol(10), wcol(12), tp)
    logf_c = _proj_foxf(u, w_in[:, cols[11][0]:cols[11][1]], w["b_fox_f"], tp)
    gates = _proj_gate(u, wcol(13), tp)

    if past is None:
        y_a, s_hgrn_new = _hgrn_scan(logf_a, k_a, q_a, v_a, z_a, w["g_hgrn"], s_hgrn, batch, seq)
        y_b, s_ret_new = _ret_scan(q_b, k_b, v_b, z_b, w["g_ret"], s_ret, batch, seq,
                                   min(RET_CHUNK, seq), min(RET_CHUNK, seq))
        f_row = _fcum(logf_c.reshape(batch, seq, H_C).transpose(0, 2, 1)).reshape(batch, H_C, 1, seq)
        y_c = _fox_prompt(q_c, k_c16, v_c16, f_row, z_c, batch, seq)
    else:
        cache_k, cache_v, lf_t, page_table = past
        sp = HGRN_CHUNK
        pad = lambda a: _pad_tokens(a, batch, seq, sp)
        y_a, s_hgrn_new = _hgrn_scan(pad(logf_a), pad(k_a), pad(q_a), pad(v_a), pad(z_a),
                                     w["g_hgrn"], s_hgrn, batch, sp)
        y_a = _unpad_tokens(y_a, batch, seq, sp)
        y_b, s_ret_new = _ret_scan(pad(q_b), pad(k_b), pad(v_b), pad(z_b), w["g_ret"], s_ret,
                                   batch, sp, sp, seq)
        y_b = _unpad_tokens(y_b, batch, seq, sp)

        ps = cache_k.shape[2]
        new_pad = LANES // H_C
        hn = _new_bias(_pad_tokens(logf_c, batch, seq, ps).reshape(batch, ps, H_C))
        hn_rows = hn[:, :new_pad].reshape(batch, 1, new_pad * H_C)
        to_rows = lambda a: a.reshape(batch, seq * H_C, DH_C)
        new_rows = lambda a: _pad_tokens(a, batch, seq, new_pad).reshape(batch, new_pad * H_C, DH_C)
        y_rows = _fox_decode(to_rows(q_c), cache_k, cache_v, lf_t, new_rows(k_c16), new_rows(v_c16),
                             hn_rows, to_rows(z_c), page_table, layer, seq, new_pad)
        y_c = y_rows.reshape(m, W_C)

    bf = lambda name: w[name].astype(BF16)
    outs = _merge(y_a, y_b, y_c, gates, x2, gate, bf("w_proj_a"), bf("w_proj_b"), bf("w_proj_c"),
                  bf("w_out"), w["g_final"], min(256, m), seq, final)
    return outs, s_hgrn_new, s_ret_new, k_c32, v_c32, logf_c


def kernel(x_prompt, x_sample, c_prompt, c_sample, cache_k, cache_v, cache_logf, state_hgrn, state_ret,
           page_table, w_ada, b_ada, g_norm, w_in, b_fox_f, lb_logits, g_hgrn, g_ret, w_proj_a, w_proj_b,
           w_proj_c, w_out, g_final):
    bp, lp, d = x_prompt.shape
    bs, ls, _ = x_sample.shape
    depth = w_in.shape[0]
    n_pages = page_table.shape[1]
    past_len = n_pages * cache_k.shape[2]
    pos_p = jnp.arange(lp, dtype=jnp.int32)
    pos_s = past_len + jnp.arange(ls, dtype=jnp.int32)

    n_c = bp + bs
    rows = -(-n_c // 8) * 8
    c_all = jnp.pad(jnp.concatenate([c_prompt, c_sample], axis=0), ((0, rows - n_c), (0, 0)))
    mod = _ada_mod(c_all, w_ada, b_ada)

    lf_t = jnp.transpose(cache_logf, (1, 0, 3, 2))
    zero_hgrn = jnp.zeros((bp, H_A, DK_A, DV_A), F32)
    zero_ret = jnp.zeros((bp, H_B, DK_B, DV_B), F32)
    xp = x_prompt.reshape(bp * lp, d)
    xs = x_sample.reshape(bs * ls, d)
    acc = {k: [] for k in ("kp", "vp", "fp", "hp", "rp", "ks", "vs", "fs", "hs", "rs")}
    yp = ys = None
    for l in range(depth):
        w = dict(g_norm=g_norm[l], w_in=w_in[l], b_fox_f=b_fox_f[l], lb_logits=lb_logits,
                 g_hgrn=g_hgrn[l], g_ret=g_ret[l], w_proj_a=w_proj_a[l], w_proj_b=w_proj_b[l],
                 w_proj_c=w_proj_c[l], w_out=w_out[l], g_final=g_final)
        final = l == depth - 1
        outs, h, r, k, v, f = _layer(l, xp, bp, lp, mod[l, :bp], pos_p, w, zero_hgrn, zero_ret, None, final)
        xp = outs[0]
        if final:
            yp = outs[1]
        acc["kp"].append(k.reshape(bp, lp, H_C, DH_C)); acc["vp"].append(v.reshape(bp, lp, H_C, DH_C))
        acc["fp"].append(f.reshape(bp, lp, H_C)); acc["hp"].append(h); acc["rp"].append(r)

        outs, h, r, k, v, f = _layer(l, xs, bs, ls, mod[l, bp:bp + bs], pos_s, w, state_hgrn[:, l],
                                     state_ret[:, l], (cache_k, cache_v, lf_t, page_table), final)
        xs = outs[0]
        if final:
            ys = outs[1]
        acc["ks"].append(k.reshape(bs, ls, H_C, DH_C)); acc["vs"].append(v.reshape(bs, ls, H_C, DH_C))
        acc["fs"].append(f.reshape(bs, ls, H_C)); acc["hs"].append(h); acc["rs"].append(r)

    st = lambda name: jnp.stack(acc[name], axis=1)
    return (yp.reshape(bp, lp, d), ys.reshape(bs, ls, d),
            st("kp"), st("vp"), st("fp"), st("hp"), st("rp"),
            st("ks"), st("vs"), st("fs"), st("hs"), st("rs"))
```

```python
import functools
import math

import numpy as np
import jax
import jax.numpy as jnp
from jax import lax
from jax.experimental import pallas as pl
from jax.experimental.pallas import tpu as pltpu

F32 = jnp.float32
BF16 = jnp.bfloat16

H_A, DK_A, DV_A = 8, 128, 128
H_B, DK_B, DV_B = 4, 128, 256
H_C, DH_C = 8, 128
W_A = H_A * DV_A
W_B = H_B * DV_B
W_C = H_C * DH_C
N_BRANCH = 3
ROPE_BASE = 10000.0
RET_DECAY_BASE = 5.0
NORM_EPS = 1e-6
MASK_VALUE = -1e30
LOG2E = math.log2(math.e)

LANES = 128
HGRN_CHUNK = 128
HGRN_HEADS_PER_STEP = 8
PROJ_ROWS = 2048
DECODE_PAGES_PER_GROUP = 8
DECODE_SLOTS = 3
RET_CHUNK = 256
SHORT_CHUNK = 128
FOX_BLOCK = 512
FOX_SOFTMAX_ROWS = 128
VMEM_LIMIT = 56 * 1024 * 1024


def _cparams(sem):
    return pltpu.CompilerParams(dimension_semantics=sem, vmem_limit_bytes=VMEM_LIMIT)


def _dot(a, b):
    return jnp.dot(a, b, preferred_element_type=F32)


def _dot_nt(a, b):
    return lax.dot_general(a, b, (((1,), (1,)), ((), ())), preferred_element_type=F32)


def _split3(x):
    hi = x.astype(BF16)
    r = x - hi.astype(F32)
    mid = r.astype(BF16)
    lo = (r - mid.astype(F32)).astype(BF16)
    return hi, mid, lo


def _dot_exact_lhs(t, x, terms=3):
    parts = _split3(x)[:terms]
    out = _dot(t, parts[0])
    for part in parts[1:]:
        out = out + _dot(t, part)
    return out


def _dot_exact_rhs(x, t):
    hi, mid, lo = _split3(x)
    return _dot(hi, t) + _dot(mid, t) + _dot(lo, t)


def _sigmoid(x):
    return 1.0 / (1.0 + jnp.exp(-x))


def _silu(x):
    return x * _sigmoid(x)


def _log_sigmoid(x):
    return jnp.minimum(x, 0.0) - jnp.log(1.0 + jnp.exp(-jnp.abs(x)))


def _ada_body(c_ref, w_ref, b_ref, o_ref):
    c = c_ref[...]
    w = w_ref[0]
    c_hi = c.astype(BF16)
    c_lo = (c - c_hi.astype(F32)).astype(BF16)
    w_hi = w.astype(BF16)
    w_lo = (w - w_hi.astype(F32)).astype(BF16)
    o_ref[0] = _dot(c_hi, w_hi) + _dot(c_hi, w_lo) + _dot(c_lo, w_hi) + b_ref[0]


def _ada_mod(c_all, w_ada, b_ada):
    depth, d, d3 = w_ada.shape
    rows = c_all.shape[0]
    return pl.pallas_call(
        _ada_body,
        grid=(depth, d3 // d),
        in_specs=[pl.BlockSpec((rows, d), lambda l, j: (0, 0)),
                  pl.BlockSpec((1, d, d), lambda l, j: (l, 0, j)),
                  pl.BlockSpec((1, 1, d), lambda l, j: (l, 0, j))],
        out_specs=pl.BlockSpec((1, rows, d), lambda l, j: (l, 0, j)),
        out_shape=jax.ShapeDtypeStruct((depth, rows, d3), F32),
        compiler_params=_cparams(("arbitrary", "arbitrary")),
        name="ada_mod",
    )(c_all, w_ada, b_ada.reshape(depth, 1, d3))


def _mod_spec(mod, tm, seq):
    if mod.ndim == 3:
        return pl.BlockSpec((None, 1, mod.shape[-1]), lambda i: ((i * tm) // seq, 0, 0))
    return pl.BlockSpec((tm, mod.shape[-1]), lambda i: (i, 0))


def _mod_arrays(mod_l, batch, seq, tm, d):
    parts = [mod_l[:, k * d:(k + 1) * d] for k in range(3)]
    if seq % tm == 0:
        return [p.reshape(batch, 1, d) for p in parts]
    return [jnp.broadcast_to(p[:, None, :], (batch, seq, d)).reshape(batch * seq, d) for p in parts]


def _norm_body(x_ref, g_ref, sh_ref, sc_ref, u_ref):
    x = x_ref[...]
    ms = jnp.mean(x * x, axis=-1, keepdims=True)
    y = x * lax.rsqrt(ms + NORM_EPS) * g_ref[...]
    u_ref[...] = (y * (1.0 + sc_ref[...]) + sh_ref[...]).astype(BF16)


def _norm_mod(x2, g, shift, scale, tm, seq):
    m, d = x2.shape
    return pl.pallas_call(
        _norm_body,
        grid=(m // tm,),
        in_specs=[pl.BlockSpec((tm, d), lambda i: (i, 0)),
                  pl.BlockSpec((1, d), lambda i: (0, 0)),
                  _mod_spec(shift, tm, seq), _mod_spec(scale, tm, seq)],
        out_specs=pl.BlockSpec((tm, d), lambda i: (i, 0)),
        out_shape=jax.ShapeDtypeStruct((m, d), BF16),
        compiler_params=_cparams(("parallel",)),
        name="norm_mod",
    )(x2, g.reshape(1, d), shift, scale)


def _proj_hgrn_body(layer, u_ref, wf_ref, wq_ref, wi_ref, wz_ref, lbl_ref,
                    logf_ref, k_ref, q_ref, v_ref, z_ref):
    u = u_ref[...]
    lbl = lbl_ref[...]
    e = jnp.exp(lbl - jnp.max(lbl, axis=0, keepdims=True))
    sm = e / jnp.sum(e, axis=0, keepdims=True)
    lb = jnp.zeros_like(sm[0:1])
    for r in range(1, layer + 1):
        lb = lb + sm[r:r + 1]
    fl = _dot(u, wf_ref[...])
    logf_ref[...] = jnp.log(lb + (1.0 - lb) * _sigmoid(fl))
    k_ref[...] = (1.0 - lb) * _sigmoid(-fl)
    q_ref[...] = _silu(_dot(u, wq_ref[...])) * (DK_A ** -0.5)
    v_ref[...] = _dot(u, wi_ref[...]).astype(BF16)
    z_ref[...] = _silu(_dot(u, wz_ref[...])).astype(BF16)


def _proj_hgrn(u, wf, wq, wi, wz, lb_logits, layer, tm, tn=256):
    m, d = u.shape
    n = wf.shape[1]
    depth = lb_logits.shape[0]
    wspec = pl.BlockSpec((d, tn), lambda j, i: (0, j))
    ospec = pl.BlockSpec((tm, tn), lambda j, i: (i, j))
    return pl.pallas_call(
        functools.partial(_proj_hgrn_body, layer),
        grid=(n // tn, m // tm),
        in_specs=[pl.BlockSpec((tm, d), lambda j, i: (i, 0)), wspec, wspec, wspec, wspec,
                  pl.BlockSpec((depth, tn), lambda j, i: (0, j))],
        out_specs=[ospec] * 5,
        out_shape=[jax.ShapeDtypeStruct((m, n), F32), jax.ShapeDtypeStruct((m, n), F32),
                   jax.ShapeDtypeStruct((m, n), F32), jax.ShapeDtypeStruct((m, n), BF16),
                   jax.ShapeDtypeStruct((m, n), BF16)],
        compiler_params=_cparams(("parallel", "parallel")),
        name="proj_hgrn",
    )(u, wf, wq, wi, wz, lb_logits)


def _rope(x, cos, sin_signed):
    return x * cos + pltpu.roll(x, DK_B // 2, axis=1) * sin_signed


def _proj_ret_body(u_ref, wq_ref, wk_ref, wv_ref, wz_ref, cos_ref, sin_ref,
                   q_ref, k_ref, v_ref, z_ref):
    u = u_ref[...]
    cos = cos_ref[...]
    sin = sin_ref[...]
    q_ref[...] = _rope(_dot(u, wq_ref[...]), cos, sin) * (DK_B ** -0.5)
    k_ref[...] = _rope(_dot(u, wk_ref[...]), cos, sin)
    v_ref[...] = _dot(u, wv_ref[...]).astype(BF16)
    z_ref[...] = _silu(_dot(u, wz_ref[...])).astype(BF16)


def _proj_ret(u, wq, wk, wv, wz, cos, sin, tm):
    m, d = u.shape
    n_rope_tiles = cos.shape[0] // tm
    qk_w = pl.BlockSpec((d, DK_B), lambda h, i: (0, h))
    vz_w = pl.BlockSpec((d, DV_B), lambda h, i: (0, h))
    rope_spec = pl.BlockSpec((tm, DK_B), lambda h, i: (i % n_rope_tiles, 0))
    qk_o = pl.BlockSpec((tm, DK_B), lambda h, i: (i, h))
    vz_o = pl.BlockSpec((tm, DV_B), lambda h, i: (i, h))
    return pl.pallas_call(
        _proj_ret_body,
        grid=(H_B, m // tm),
        in_specs=[pl.BlockSpec((tm, d), lambda h, i: (i, 0)), qk_w, qk_w, vz_w, vz_w,
                  rope_spec, rope_spec],
        out_specs=[qk_o, qk_o, vz_o, vz_o],
        out_shape=[jax.ShapeDtypeStruct((m, H_B * DK_B), F32), jax.ShapeDtypeStruct((m, H_B * DK_B), F32),
                   jax.ShapeDtypeStruct((m, W_B), BF16), jax.ShapeDtypeStruct((m, W_B), BF16)],
        compiler_params=_cparams(("parallel", "parallel")),
        name="proj_ret",
    )(u, wq, wk, wv, wz, cos, sin)


def _proj_fox_body(u_ref, wq_ref, wk_ref, wv_ref, wz_ref,
                   q16_ref, k32_ref, k16_ref, v32_ref, v16_ref, z_ref):
    u = u_ref[...]
    q16_ref[...] = _dot(u, wq_ref[...]).astype(BF16)
    k = _dot(u, wk_ref[...])
    k32_ref[...] = k
    k16_ref[...] = k.astype(BF16)
    v = _dot(u, wv_ref[...])
    v32_ref[...] = v
    v16_ref[...] = v.astype(BF16)
    z_ref[...] = _silu(_dot(u, wz_ref[...])).astype(BF16)


def _proj_fox(u, wq, wk, wv, wz, tm, tn=256):
    m, d = u.shape
    n = wq.shape[1]
    wspec = pl.BlockSpec((d, tn), lambda j, i: (0, j))
    ospec = pl.BlockSpec((tm, tn), lambda j, i: (i, j))
    shp = lambda dt: jax.ShapeDtypeStruct((m, n), dt)
    return pl.pallas_call(
        _proj_fox_body,
        grid=(n // tn, m // tm),
        in_specs=[pl.BlockSpec((tm, d), lambda j, i: (i, 0)), wspec, wspec, wspec, wspec],
        out_specs=[ospec] * 6,
        out_shape=[shp(BF16), shp(F32), shp(BF16), shp(F32), shp(BF16), shp(BF16)],
        compiler_params=_cparams(("parallel", "parallel")),
        name="proj_fox",
    )(u, wq, wk, wv, wz)


def _proj_foxf_body(u_ref, whi_ref, wlo_ref, b_ref, o_ref):
    u = u_ref[...]
    cf = _dot(u, whi_ref[...]) + _dot(u, wlo_ref[...]) + b_ref[...]
    o_ref[...] = _log_sigmoid(cf)[:, :H_C]


def _proj_foxf(u, wf, bf, tm):
    m, d = u.shape
    wpad = jnp.pad(wf, ((0, 0), (0, LANES - H_C)))
    whi = wpad.astype(BF16)
    wlo = (wpad - whi.astype(F32)).astype(BF16)
    bpad = jnp.pad(bf.reshape(1, H_C), ((0, 0), (0, LANES - H_C)))
    wspec = pl.BlockSpec((d, LANES), lambda i: (0, 0))
    return pl.pallas_call(
        _proj_foxf_body,
        grid=(m // tm,),
        in_specs=[pl.BlockSpec((tm, d), lambda i: (i, 0)), wspec, wspec,
                  pl.BlockSpec((1, LANES), lambda i: (0, 0))],
        out_specs=pl.BlockSpec((tm, H_C), lambda i: (i, 0)),
        out_shape=jax.ShapeDtypeStruct((m, H_C), F32),
        compiler_params=_cparams(("parallel",)),
        name="proj_foxf",
    )(u, whi, wlo, bpad)


def _proj_gate_body(u_ref, w_ref, o_ref):
    o_ref[...] = _sigmoid(_dot(u_ref[...], w_ref[...])).astype(BF16)


def _proj_gate(u, wg, tm, tn=512):
    m, d = u.shape
    n = wg.shape[1]
    return pl.pallas_call(
        _proj_gate_body,
        grid=(n // tn, m // tm),
        in_specs=[pl.BlockSpec((tm, d), lambda j, i: (i, 0)),
                  pl.BlockSpec((d, tn), lambda j, i: (0, j))],
        out_specs=pl.BlockSpec((tm, tn), lambda j, i: (i, j)),
        out_shape=jax.ShapeDtypeStruct((m, n), BF16),
        compiler_params=_cparams(("parallel", "parallel")),
        name="proj_gate",
    )(u, wg)


def _hgrn_levels(c, n_real):
    return [c >> (i + 1) for i in range(int(math.log2(c))) if (c >> (i + 1)) < n_real]


def _chunk_rows(ref, cols, c):
    x = ref[:, cols]
    if x.shape[0] < c:
        x = jnp.concatenate([x, jnp.zeros((c - x.shape[0], x.shape[1]), x.dtype)], axis=0)
    return x


def _hgrn_tables(c, n_real):
    t = np.arange(c)[:, None]
    u = np.arange(c)[None, :]
    mats, masks = [], []
    for w in _hgrn_levels(c, n_real):
        r = (t // (2 * w)) * (2 * w) + w
        upper = (t % (2 * w)) >= w
        mats.append(np.where(upper, (u >= r) & (u <= t), (u > t) & (u < r)))
        masks.append((t // (2 * w)) == (u // (2 * w)))
    mats.append(u <= t)
    mats.append(u > t)
    masks.append(t == u)
    return (jnp.asarray(np.concatenate(mats, 0).astype(np.float32), BF16),
            jnp.asarray(np.stack(masks, 0).astype(np.float32), F32))


def _hgrn_body(c, nc, hp, n_real, logf_ref, k_ref, q_ref, v_ref, z_ref, g_ref, s0_ref, t_ref, mask_ref,
               y_ref, sout_ref, st_ref):
    ci = pl.program_id(2)

    @pl.when(ci == 0)
    def _():
        for hh in range(hp):
            st_ref[hh] = s0_ref[0, hh].T

    levels = _hgrn_levels(c, n_real)
    nl = len(levels)
    logf = _chunk_rows(logf_ref, slice(None), c)
    e_all = jnp.exp(_dot_exact_lhs(t_ref[...], logf, terms=2))
    row = lax.broadcasted_iota(jnp.int32, (c, DK_A), 0)
    g = g_ref[...]
    n_out = y_ref.shape[0]
    for hh in range(hp):
        cols = slice(hh * DK_A, (hh + 1) * DK_A)
        q = _chunk_rows(q_ref, cols, c)
        k = _chunk_rows(k_ref, cols, c)
        v = _chunk_rows(v_ref, cols, c).astype(BF16)
        a = mask_ref[nl] * _dot_nt(q.astype(BF16), k.astype(BF16))
        for li, w in enumerate(levels):
            e_w = e_all[li * c:(li + 1) * c, cols]
            upper = (row & w) != 0
            q_w = jnp.where(upper, q * e_w, 0.0).astype(BF16)
            k_w = jnp.where(upper, 0.0, k * e_w).astype(BF16)
            a = a + mask_ref[li] * _dot_nt(q_w, k_w)
        e_b = e_all[nl * c:(nl + 1) * c, cols]
        e_end = e_all[(nl + 1) * c:(nl + 2) * c, cols]
        st = st_ref[hh]
        o = _dot(a.astype(BF16), v) + _dot_nt((q * e_b).astype(BF16), st.astype(BF16))
        v_t = v.astype(F32).T.astype(BF16)
        st_ref[hh] = st * e_b[c - 1:c, :] + _dot(v_t, (k * e_end).astype(BF16))
        ms = jnp.mean(o * o, axis=-1, keepdims=True)
        y = o * lax.rsqrt(ms + NORM_EPS) * g * _chunk_rows(z_ref, cols, c)
        y_ref[:, cols] = y[:n_out].astype(y_ref.dtype)

    @pl.when(ci == nc - 1)
    def _():
        for hh in range(hp):
            sout_ref[0, hh] = st_ref[hh].T


def _hgrn_scan(logf, k, q, v, z, g, s0, batch, seq):
    c = HGRN_CHUNK
    hp = HGRN_HEADS_PER_STEP
    rows = min(seq, c)
    nc = seq // rows
    m = batch * seq
    tmat, masks = _hgrn_tables(c, rows)
    tok = pl.BlockSpec((rows, hp * DK_A), lambda b, h, ci: (b * nc + ci, h))
    st_spec = pl.BlockSpec((1, hp, DK_A, DV_A), lambda b, h, ci: (b, h, 0, 0))
    return pl.pallas_call(
        functools.partial(_hgrn_body, c, nc, hp, rows),
        grid=(batch, H_A // hp, nc),
        in_specs=[tok, tok, tok, tok, tok,
                  pl.BlockSpec((1, DV_A), lambda b, h, ci: (0, 0)),
                  st_spec,
                  pl.BlockSpec(tmat.shape, lambda b, h, ci: (0, 0)),
                  pl.BlockSpec(masks.shape, lambda b, h, ci: (0, 0, 0))],
        out_specs=[tok, st_spec],
        out_shape=[jax.ShapeDtypeStruct((m, W_A), BF16 if rows == c else F32),
                   jax.ShapeDtypeStruct((batch, H_A, DK_A, DV_A), F32)],
        scratch_shapes=[pltpu.VMEM((hp, DV_A, DK_A), F32)],
        compiler_params=_cparams(("parallel", "parallel", "arbitrary")),
        name="hgrn_scan",
    )(logf, k, q, v, z, g.reshape(1, DV_A), s0, tmat, masks)


def _ret_body(c, c_eff, nc, lg_ref, q_ref, k_ref, v_ref, z_ref, g_ref, s0_ref,
              y_ref, sout_ref, s_ref):
    ci = pl.program_id(1)

    @pl.when(ci == 0)
    def _():
        s_ref[...] = s0_ref[0]

    t = lax.broadcasted_iota(jnp.int32, (c, DK_B), 0).astype(F32)
    rel = (lax.broadcasted_iota(jnp.int32, (c, c), 0)
           - lax.broadcasted_iota(jnp.int32, (c, c), 1)).astype(F32)
    g = g_ref[...]
    for h in range(H_B):
        qk_cols = slice(h * DK_B, (h + 1) * DK_B)
        vz_cols = slice(h * DV_B, (h + 1) * DV_B)
        lg = lg_ref[h]
        q = _chunk_rows(q_ref, qk_cols, c)
        k = _chunk_rows(k_ref, qk_cols, c)
        v = _chunk_rows(v_ref, vz_cols, c).astype(BF16)
        q_dec = jnp.exp((t + 1.0) * lg)
        k_dec = jnp.where(t < c_eff, jnp.exp(jnp.maximum(c_eff - 1.0 - t, 0.0) * lg), 0.0)
        intra = jnp.where(rel >= 0.0, jnp.exp(jnp.maximum(rel, 0.0) * lg), 0.0)
        a = _dot_nt(q.astype(BF16), k.astype(BF16)) * intra
        s = s_ref[h]
        o = _dot(a.astype(BF16), v) + _dot((q * q_dec).astype(BF16), s.astype(BF16))
        kd_t = (k * k_dec).T.astype(BF16)
        chunk_dec = jnp.exp(jnp.full((1, DV_B), c_eff, F32) * lg)
        s_ref[h] = chunk_dec * s + _dot(kd_t, v)
        ms = jnp.mean(o * o, axis=-1, keepdims=True)
        y = o * lax.rsqrt(ms + NORM_EPS) * g * _chunk_rows(z_ref, vz_cols, c)
        y_ref[:, vz_cols] = y[:y_ref.shape[0]].astype(y_ref.dtype)

    @pl.when(ci == nc - 1)
    def _():
        sout_ref[0] = s_ref[...]


def _ret_scan(q, k, v, z, g, s0, batch, seq):
    c = RET_CHUNK if seq >= RET_CHUNK else SHORT_CHUNK
    c_eff = min(seq, c)
    nc = seq // c_eff
    m = batch * seq
    log_gamma = jnp.log1p(-jnp.power(2.0, -RET_DECAY_BASE - jnp.arange(H_B, dtype=F32)))
    qk = pl.BlockSpec((c_eff, H_B * DK_B), lambda b, ci: (b * nc + ci, 0))
    vz = pl.BlockSpec((c_eff, W_B), lambda b, ci: (b * nc + ci, 0))
    st_spec = pl.BlockSpec((1, H_B, DK_B, DV_B), lambda b, ci: (b, 0, 0, 0))
    return pl.pallas_call(
        functools.partial(_ret_body, c, c_eff, nc),
        grid=(batch, nc),
        in_specs=[pl.BlockSpec(memory_space=pltpu.SMEM), qk, qk, vz, vz,
                  pl.BlockSpec((1, DV_B), lambda b, ci: (0, 0)), st_spec],
        out_specs=[vz, st_spec],
        out_shape=[jax.ShapeDtypeStruct((m, W_B), BF16 if c_eff == c else F32),
                   jax.ShapeDtypeStruct((batch, H_B, DK_B, DV_B), F32)],
        scratch_shapes=[pltpu.VMEM((H_B, DK_B, DV_B), F32)],
        compiler_params=_cparams(("parallel", "arbitrary")),
        name="ret_scan",
    )(log_gamma, q, k, v, z, g.reshape(1, DV_B), s0)


def _fcum_body(nblk, x_ref, o_ref):
    r = lax.broadcasted_iota(jnp.int32, (LANES, LANES), 0)
    cc = lax.broadcasted_iota(jnp.int32, (LANES, LANES), 1)
    tri = jnp.where(r <= cc, 1.0, 0.0).astype(BF16)
    carry = jnp.zeros((H_C, 1), F32)
    for blk in range(nblk):
        x = x_ref[0, :, blk * LANES:(blk + 1) * LANES]
        y = _dot_exact_rhs(x, tri) + carry
        o_ref[0, :, blk * LANES:(blk + 1) * LANES] = y * LOG2E
        carry = y[:, LANES - 1:LANES]


def _fcum(logf_t):
    b, h, seq = logf_t.shape
    spec = pl.BlockSpec((1, h, seq), lambda i: (i, 0, 0))
    return pl.pallas_call(
        functools.partial(_fcum_body, seq // LANES),
        grid=(b,),
        in_specs=[spec], out_specs=spec,
        out_shape=jax.ShapeDtypeStruct((b, h, seq), F32),
        compiler_params=_cparams(("parallel",)),
        name="fox_fcum",
    )(logf_t)


def _fold_lanes(x, op):
    out = x[:, 0:LANES]
    for c in range(1, x.shape[1] // LANES):
        out = op(out, x[:, c * LANES:(c + 1) * LANES])
    return out


def _fox_prompt_body(tq, q_ref, k_ref, v_ref, fq_ref, fk_ref, z_ref, o_ref, fqc_ref, t_ref, p_ref, st_ref):
    i = pl.program_id(2)
    q = (q_ref[...].astype(F32) * (DH_C ** -0.5 * LOG2E)).astype(BF16)
    fqc_ref[...] = jnp.broadcast_to(fq_ref[0, 0], (LANES, tq)).T
    m_ref, l_ref, a_ref, acc_ref = st_ref.at[0], st_ref.at[1], st_ref.at[2], st_ref.at[3]
    m_ref[...] = jnp.full((tq, LANES), MASK_VALUE, F32)
    l_ref[...] = jnp.zeros((tq, LANES), F32)
    a_ref[...] = jnp.ones((tq, LANES), F32)
    acc_ref[...] = jnp.zeros((tq, DH_C), F32)

    def scores(j):
        start = pl.multiple_of(j * tq, tq)
        return _dot_nt(q, k_ref[pl.ds(start, tq), :]) - fk_ref[0, 0, :, pl.ds(start, tq)]

    def softmax(slot, diagonal):
        for r in range(tq // FOX_SOFTMAX_ROWS):
            rows = slice(r * FOX_SOFTMAX_ROWS, (r + 1) * FOX_SOFTMAX_ROWS)
            t = t_ref[slot, rows, :]
            if diagonal:
                rr = lax.broadcasted_iota(jnp.int32, t.shape, 0) + r * FOX_SOFTMAX_ROWS
                cc = lax.broadcasted_iota(jnp.int32, t.shape, 1)
                t = jnp.where(rr >= cc, t, MASK_VALUE)
            fq = fqc_ref[rows, :]
            m_prev = m_ref[rows, :]
            m_new = jnp.maximum(m_prev, jnp.max(_fold_lanes(t, jnp.maximum), axis=-1, keepdims=True) + fq)
            p = jnp.exp2(t + jnp.concatenate([fq - m_new] * (tq // LANES), axis=1))
            alpha = jnp.exp2(m_prev - m_new)
            l_ref[rows, :] = alpha * l_ref[rows, :] + jnp.sum(_fold_lanes(p, jnp.add), axis=-1, keepdims=True)
            a_ref[rows, :] = alpha
            m_ref[rows, :] = m_new
            p_ref[slot, rows, :] = p.astype(BF16)

    def weighted(j, slot):
        start = pl.multiple_of(j * tq, tq)
        acc_ref[...] = a_ref[...] * acc_ref[...] + _dot(p_ref[slot], v_ref[pl.ds(start, tq), :])

    t_ref[0] = scores(0)
    p_ref[1] = jnp.zeros((tq, tq), BF16)

    def trip(j, carry):
        slot = j % 2
        weighted(jnp.maximum(j - 1, 0), 1 - slot)
        softmax(slot, False)
        t_ref[1 - slot] = scores(j + 1)
        return carry

    lax.fori_loop(0, i, trip, 0)
    slot = i % 2
    weighted(jnp.maximum(i - 1, 0), 1 - slot)
    softmax(slot, True)
    weighted(i, slot)
    o_ref[...] = (acc_ref[...] / l_ref[...] * z_ref[...]).astype(BF16)


def _fox_prompt(q16, k16, v16, f_row, z, batch, seq):
    tq = min(FOX_BLOCK, seq)
    nq = seq // tq
    m = batch * seq
    qspec = pl.BlockSpec((tq, DH_C), lambda b, h, i: (b * nq + i, h))
    kvspec = pl.BlockSpec((seq, DH_C), lambda b, h, i: (b, h))
    return pl.pallas_call(
        functools.partial(_fox_prompt_body, tq),
        grid=(batch, H_C, nq),
        in_specs=[qspec, kvspec, kvspec,
                  pl.BlockSpec((1, 1, 1, tq), lambda b, h, i: (b, h, 0, i)),
                  pl.BlockSpec((1, 1, 1, seq), lambda b, h, i: (b, h, 0, 0)),
                  qspec],
        out_specs=qspec,
        out_shape=jax.ShapeDtypeStruct((m, W_C), BF16),
        scratch_shapes=[pltpu.VMEM((tq, LANES), F32), pltpu.VMEM((2, tq, tq), F32),
                        pltpu.VMEM((2, tq, tq), BF16), pltpu.VMEM((4, tq, LANES), F32)],
        compiler_params=_cparams(("parallel", "parallel", "arbitrary")),
        name="fox_prompt",
    )(q16, k16, v16, f_row, f_row, z)


def _tri_f32(n, kind):
    r = lax.broadcasted_iota(jnp.int32, (n, n), 0)
    c = lax.broadcasted_iota(jnp.int32, (n, n), 1)
    cond = (c > r) if kind == "after" else (c <= r)
    return jnp.where(cond, 1.0, 0.0).astype(BF16)


def _new_bias_body(n, x_ref, o_ref):
    o_ref[0] = -_dot_exact_lhs(_tri_f32(n, "upto"), x_ref[0])


def _new_bias(logf_new_pad):
    nb, n, h = logf_new_pad.shape
    spec = pl.BlockSpec((1, n, h), lambda b: (b, 0, 0))
    return pl.pallas_call(
        functools.partial(_new_bias_body, n),
        grid=(nb,),
        in_specs=[spec], out_specs=spec,
        out_shape=jax.ShapeDtypeStruct((nb, n, h), F32),
        compiler_params=_cparams(("parallel",)),
        name="fox_new_bias",
    )(logf_new_pad)


def _fox_decode_body(n_groups, pg, ps, n_new, new_pad, layer, n_pages, pt_ref, q_ref, k_hbm, v_hbm, lf_hbm,
                     rep_ref, kn_ref, vn_ref, hn_ref, z_ref, o_ref, kbuf, vbuf, lfbuf, sem):
    b = pl.program_id(0)
    scale = DH_C ** -0.5
    rows = H_C * n_new
    cols = ps * H_C
    n_slots = kbuf.shape[0]

    def group_copies(g, slot):
        out = []
        for i in range(pg):
            page = pt_ref[b, n_pages - 1 - (g * pg + i)]
            out.append(pltpu.make_async_copy(k_hbm.at[page, layer], kbuf.at[slot, i], sem.at[0, slot]))
            out.append(pltpu.make_async_copy(v_hbm.at[page, layer], vbuf.at[slot, i], sem.at[1, slot]))
            out.append(pltpu.make_async_copy(lf_hbm.at[layer, page], lfbuf.at[slot, i], sem.at[2, slot]))
        return out

    for g0 in range(min(n_slots - 1, n_groups)):
        for cp in group_copies(g0, g0):
            cp.start()

    def group(g, state):
        m_prev, l_prev, acc_prev, carry = state
        slot = g % n_slots
        ahead = g + n_slots - 1

        @pl.when(ahead < n_groups)
        def _():
            for cp in group_copies(ahead, ahead % n_slots):
                cp.start()

        for cp in group_copies(g, slot):
            cp.wait()

        lf = jnp.concatenate([lfbuf[slot, i] for i in range(pg)], axis=0)
        t_from = lax.broadcasted_iota(jnp.int32, (ps, ps), 0)
        t_to = lax.broadcasted_iota(jnp.int32, (ps, ps), 1)
        later = jnp.where(t_from > t_to, 1.0, 0.0).astype(BF16)
        within = _dot_exact_rhs(lf, later)
        total = jnp.sum(lf, axis=-1, keepdims=True)
        bias = []
        for i in range(pg):
            bias.append(within[i * H_C:(i + 1) * H_C] + carry)
            carry = carry + total[i * H_C:(i + 1) * H_C]
        spread = _dot_exact_rhs(jnp.concatenate(bias, axis=0), rep_ref[...])
        other_head = jnp.where(lax.broadcasted_iota(jnp.int32, (H_C, cols), 0)
                               == lax.broadcasted_iota(jnp.int32, (H_C, cols), 1) % H_C, 0.0, MASK_VALUE)

        q = q_ref[0]
        s_list = []
        for i in range(pg):
            kf = kbuf[slot, i].reshape(cols, DH_C).astype(BF16)
            b8 = spread[i * H_C:(i + 1) * H_C] + other_head
            s_list.append(_dot_nt(q, kf) * scale + jnp.concatenate([b8] * n_new, axis=0))
        mx = s_list[0]
        for s in s_list[1:]:
            mx = jnp.maximum(mx, s)
        m_new = jnp.maximum(m_prev, jnp.max(_fold_lanes(mx, jnp.maximum), axis=-1, keepdims=True))
        alpha = jnp.exp(m_prev - m_new)
        psum = None
        pv = None
        for i in range(pg):
            p = jnp.exp(s_list[i] - m_new)
            contrib = _dot(p.astype(BF16), vbuf[slot, i].reshape(cols, DH_C).astype(BF16))
            psum = p if psum is None else psum + p
            pv = contrib if pv is None else pv + contrib
        l_new = alpha * l_prev + jnp.sum(_fold_lanes(psum, jnp.add), axis=-1, keepdims=True)
        return m_new, l_new, alpha * acc_prev + pv, carry

    init = (jnp.full((rows, 1), MASK_VALUE, F32), jnp.zeros((rows, 1), F32), jnp.zeros((rows, DH_C), F32),
            jnp.zeros((H_C, 1), F32))
    m_prev, l_prev, acc_prev, _ = lax.fori_loop(0, n_groups, group, init)

    ncol = new_pad * H_C
    s = _dot_nt(q_ref[0], kn_ref[0]) * scale + hn_ref[0]
    rr = lax.broadcasted_iota(jnp.int32, (rows, ncol), 0)
    cc = lax.broadcasted_iota(jnp.int32, (rows, ncol), 1)
    tok = cc // H_C
    ok = ((rr % H_C) == (cc % H_C)) & (tok <= (rr // H_C)) & (tok < n_new)
    s = jnp.where(ok, s, MASK_VALUE)
    m_new = jnp.maximum(m_prev, jnp.max(s, axis=-1, keepdims=True))
    p = jnp.exp(s - m_new)
    alpha = jnp.exp(m_prev - m_new)
    l_fin = alpha * l_prev + jnp.sum(p, axis=-1, keepdims=True)
    acc = alpha * acc_prev + _dot(p.astype(BF16), vn_ref[0])
    o_ref[0] = (acc / l_fin * z_ref[0]).astype(BF16)


def _fox_decode(q_rows, cache_k, cache_v, lf_t, k_new, v_new, hn_rows, z_rows, page_table, layer,
                n_new, new_pad):
    nb, n_pages = page_table.shape
    ps = cache_k.shape[2]
    pg = min(DECODE_PAGES_PER_GROUP, n_pages)
    n_groups = n_pages // pg
    rows = H_C * n_new
    cols = ps * H_C
    ncol = new_pad * H_C
    rep = jnp.asarray((np.arange(ps)[:, None] == np.arange(cols)[None, :] // H_C).astype(np.float32), BF16)

    hbm = pl.BlockSpec(memory_space=pltpu.HBM)
    per_b = lambda shape: pl.BlockSpec((1,) + shape, lambda b, pt: (b, 0, 0))
    grid_spec = pltpu.PrefetchScalarGridSpec(
        num_scalar_prefetch=1,
        grid=(nb,),
        in_specs=[per_b((rows, DH_C)), hbm, hbm, hbm,
                  pl.BlockSpec((ps, cols), lambda b, pt: (0, 0)),
                  per_b((ncol, DH_C)), per_b((ncol, DH_C)), per_b((1, ncol)), per_b((rows, DH_C))],
        out_specs=per_b((rows, DH_C)),
        scratch_shapes=[pltpu.VMEM((DECODE_SLOTS, pg, ps, H_C, DH_C), F32),
                        pltpu.VMEM((DECODE_SLOTS, pg, ps, H_C, DH_C), F32),
                        pltpu.VMEM((DECODE_SLOTS, pg, H_C, ps), F32),
                        pltpu.SemaphoreType.DMA((3, DECODE_SLOTS))],
    )
    return pl.pallas_call(
        functools.partial(_fox_decode_body, n_groups, pg, ps, n_new, new_pad, layer, n_pages),
        grid_spec=grid_spec,
        out_shape=jax.ShapeDtypeStruct((nb, rows, DH_C), BF16),
        compiler_params=_cparams(("arbitrary",)),
        name="fox_decode",
    )(page_table, q_rows, cache_k, cache_v, lf_t, rep, k_new, v_new, hn_rows, z_rows)


def _merge_body(final, ya_ref, yb_ref, yc_ref, g0_ref, g1_ref, g2_ref, x_ref, gate_ref,
                wa_ref, wb_ref, wc_ref, wo_ref, gf_ref, xo_ref, *maybe_y):
    merged = (g0_ref[...] * _dot(ya_ref[...], wa_ref[...])
              + g1_ref[...] * _dot(yb_ref[...], wb_ref[...])
              + g2_ref[...] * _dot(yc_ref[...], wc_ref[...]))
    xn = x_ref[...] + gate_ref[...] * _dot(merged.astype(BF16), wo_ref[...])
    xo_ref[...] = xn
    if final:
        ms = jnp.mean(xn * xn, axis=-1, keepdims=True)
        maybe_y[0][...] = xn * lax.rsqrt(ms + NORM_EPS) * gf_ref[...]


def _merge(ya, yb, yc, g, x2, gate, wa, wb, wc, wo, g_final, tm, seq, final):
    m, d = x2.shape
    row = pl.BlockSpec((tm, d), lambda i: (i, 0))
    gspec = lambda k: pl.BlockSpec((tm, d), lambda i: (i, k))
    wspec = pl.BlockSpec((d, d), lambda i: (0, 0))
    out_shape = [jax.ShapeDtypeStruct((m, d), F32)]
    out_specs = [row]
    if final:
        out_shape.append(jax.ShapeDtypeStruct((m, d), F32))
        out_specs.append(row)
    return pl.pallas_call(
        functools.partial(_merge_body, final),
        grid=(m // tm,),
        in_specs=[row, row, row, gspec(0), gspec(1), gspec(2), row, _mod_spec(gate, tm, seq),
                  wspec, wspec, wspec, wspec, pl.BlockSpec((1, d), lambda i: (0, 0))],
        out_specs=out_specs,
        out_shape=out_shape,
        compiler_params=_cparams(("parallel",)),
        name="merge_out",
    )(ya, yb, yc, g, g, g, x2, gate, wa, wb, wc, wo, g_final.reshape(1, d))


def _col_offsets(d):
    sizes = (H_A * DK_A, H_A * DK_A, W_A, W_A,
             H_B * DK_B, H_B * DK_B, W_B, W_B,
             W_C, W_C, W_C, H_C, W_C,
             N_BRANCH * d)
    offs = np.concatenate([[0], np.cumsum(sizes)])
    return [(int(offs[i]), int(offs[i + 1])) for i in range(len(sizes))]


def _rope_tables(pos):
    half = DK_B // 2
    inv = ROPE_BASE ** (-jnp.arange(half, dtype=F32) / half)
    ang = pos.astype(F32)[:, None] * inv[None, :]
    cos = jnp.cos(ang)
    sin = jnp.sin(ang)
    return jnp.concatenate([cos, cos], axis=-1), jnp.concatenate([-sin, sin], axis=-1)


def _pad_tokens(a, batch, seq, seq_pad):
    n = a.shape[-1]
    return jnp.pad(a.reshape(batch, seq, n), ((0, 0), (0, seq_pad - seq), (0, 0))).reshape(batch * seq_pad, n)


def _layer(layer, x2, batch, seq, mod_l, pos, w, s_hgrn, s_ret, past, final):
    m, d = x2.shape
    tm = min(512, m)
    shift, scale, gate = _mod_arrays(mod_l, batch, seq, tm, d)
    u = _norm_mod(x2, w["g_norm"], shift, scale, tm, seq)

    cols = _col_offsets(d)
    w_in = w["w_in"]
    wcol = lambda i: w_in[:, cols[i][0]:cols[i][1]].astype(BF16)

    tp = min(PROJ_ROWS, m)
    logf_a, k_a, q_a, v_a, z_a = _proj_hgrn(u, wcol(0), wcol(1), wcol(2), wcol(3), w["lb_logits"], layer, tp)
    cos, sin = _rope_tables(pos)
    if cos.shape[0] % tp != 0:
        cos = jnp.tile(cos, (m // cos.shape[0], 1))
        sin = jnp.tile(sin, (m // sin.shape[0], 1))
    q_b, k_b, v_b, z_b = _proj_ret(u, wcol(4), wcol(5), wcol(6), wcol(7), cos, sin, tp)
    q_c, k_c32, k_c16, v_c32, v_c16, z_c = _proj_fox(u, wcol(8), wcol(9), wcol(10), wcol(12), tp)
    logf_c = _proj_foxf(u, w_in[:, cols[11][0]:cols[11][1]], w["b_fox_f"], tp)
    gates = _proj_gate(u, wcol(13), tp)

    if past is None:
        y_a, s_hgrn_new = _hgrn_scan(logf_a, k_a, q_a, v_a, z_a, w["g_hgrn"], s_hgrn, batch, seq)
        y_b, s_ret_new = _ret_scan(q_b, k_b, v_b, z_b, w["g_ret"], s_ret, batch, seq)
        f_row = _fcum(logf_c.reshape(batch, seq, H_C).transpose(0, 2, 1)).reshape(batch, H_C, 1, seq)
        y_c = _fox_prompt(q_c, k_c16, v_c16, f_row, z_c, batch, seq)
    else:
        cache_k, cache_v, lf_t, page_table = past
        f32 = lambda a: a.astype(F32)
        y_a, s_hgrn_new = _hgrn_scan(logf_a, k_a, q_a, f32(v_a), f32(z_a), w["g_hgrn"], s_hgrn, batch, seq)
        y_b, s_ret_new = _ret_scan(q_b, k_b, f32(v_b), f32(z_b), w["g_ret"], s_ret, batch, seq)
        y_a, y_b = y_a.astype(BF16), y_b.astype(BF16)

        ps = cache_k.shape[2]
        new_pad = LANES // H_C
        hn = _new_bias(_pad_tokens(logf_c, batch, seq, ps).reshape(batch, ps, H_C))
        hn_rows = hn[:, :new_pad].reshape(batch, 1, new_pad * H_C)
        to_rows = lambda a: a.reshape(batch, seq * H_C, DH_C)
        new_rows = lambda a: _pad_tokens(a, batch, seq, new_pad).reshape(batch, new_pad * H_C, DH_C)
        y_rows = _fox_decode(to_rows(q_c), cache_k, cache_v, lf_t, new_rows(k_c16), new_rows(v_c16),
                             hn_rows, to_rows(z_c), page_table, layer, seq, new_pad)
        y_c = y_rows.reshape(m, W_C)

    bf = lambda name: w[name].astype(BF16)
    outs = _merge(y_a, y_b, y_c, gates, x2, gate, bf("w_proj_a"), bf("w_proj_b"), bf("w_proj_c"),
                  bf("w_out"), w["g_final"], min(256, m), seq, final)
    return outs, s_hgrn_new, s_ret_new, k_c32, v_c32, logf_c


def kernel(x_prompt, x_sample, c_prompt, c_sample, cache_k, cache_v, cache_logf, state_hgrn, state_ret,
           page_table, w_ada, b_ada, g_norm, w_in, b_fox_f, lb_logits, g_hgrn, g_ret, w_proj_a, w_proj_b,
           w_proj_c, w_out, g_final):
    bp, lp, d = x_prompt.shape
    bs, ls, _ = x_sample.shape
    depth = w_in.shape[0]
    n_pages = page_table.shape[1]
    past_len = n_pages * cache_k.shape[2]
    pos_p = jnp.arange(lp, dtype=jnp.int32)
    pos_s = past_len + jnp.arange(ls, dtype=jnp.int32)

    n_c = bp + bs
    rows = -(-n_c // 8) * 8
    c_all = jnp.pad(jnp.concatenate([c_prompt, c_sample], axis=0), ((0, rows - n_c), (0, 0)))
    mod = _ada_mod(c_all, w_ada, b_ada)

    lf_t = jnp.transpose(cache_logf, (1, 0, 3, 2))
    zero_hgrn = jnp.zeros((bp, H_A, DK_A, DV_A), F32)
    zero_ret = jnp.zeros((bp, H_B, DK_B, DV_B), F32)
    xp = x_prompt.reshape(bp * lp, d)
    xs = x_sample.reshape(bs * ls, d)
    acc = {k: [] for k in ("kp", "vp", "fp", "hp", "rp", "ks", "vs", "fs", "hs", "rs")}
    yp = ys = None
    for l in range(depth):
        w = dict(g_norm=g_norm[l], w_in=w_in[l], b_fox_f=b_fox_f[l], lb_logits=lb_logits,
                 g_hgrn=g_hgrn[l], g_ret=g_ret[l], w_proj_a=w_proj_a[l], w_proj_b=w_proj_b[l],
                 w_proj_c=w_proj_c[l], w_out=w_out[l], g_final=g_final)
        final = l == depth - 1
        outs, h, r, k, v, f = _layer(l, xp, bp, lp, mod[l, :bp], pos_p, w, zero_hgrn, zero_ret, None, final)
        xp = outs[0]
        if final:
            yp = outs[1]
        acc["kp"].append(k.reshape(bp, lp, H_C, DH_C)); acc["vp"].append(v.reshape(bp, lp, H_C, DH_C))
        acc["fp"].append(f.reshape(bp, lp, H_C)); acc["hp"].append(h); acc["rp"].append(r)

        outs, h, r, k, v, f = _layer(l, xs, bs, ls, mod[l, bp:bp + bs], pos_s, w, state_hgrn[:, l],
                                     state_ret[:, l], (cache_k, cache_v, lf_t, page_table), final)
        xs = outs[0]
        if final:
            ys = outs[1]
        acc["ks"].append(k.reshape(bs, ls, H_C, DH_C)); acc["vs"].append(v.reshape(bs, ls, H_C, DH_C))
        acc["fs"].append(f.reshape(bs, ls, H_C)); acc["hs"].append(h); acc["rs"].append(r)

    st = lambda name: jnp.stack(acc[name], axis=1)
    return (yp.reshape(bp, lp, d), ys.reshape(bs, ls, d),
            st("kp"), st("vp"), st("fp"), st("hp"), st("rp"),
            st("ks"), st("vs"), st("fs"), st("hs"), st("rs"))
```

```python
import functools
import math

import numpy as np
import jax
import jax.numpy as jnp
from jax import lax
from jax.experimental import pallas as pl
from jax.experimental.pallas import tpu as pltpu

F32 = jnp.float32
BF16 = jnp.bfloat16

H_A, DK_A, DV_A = 8, 128, 128
H_B, DK_B, DV_B = 4, 128, 256
H_C, DH_C = 8, 128
W_A = H_A * DV_A
W_B = H_B * DV_B
W_C = H_C * DH_C
N_BRANCH = 3
ROPE_BASE = 10000.0
RET_DECAY_BASE = 5.0
NORM_EPS = 1e-6
MASK_VALUE = -1e30
LOG2E = math.log2(math.e)

LANES = 128
HGRN_CHUNK = 128
HGRN_HEADS_PER_STEP = 8
PROJ_ROWS = 2048
DECODE_PAGES_PER_GROUP = 8
DECODE_SLOTS = 4
RET_CHUNK = 256
SHORT_CHUNK = 128
FOX_BLOCK = 512
FOX_SOFTMAX_ROWS = 128
VMEM_LIMIT = 56 * 1024 * 1024


def _cparams(sem):
    return pltpu.CompilerParams(dimension_semantics=sem, vmem_limit_bytes=VMEM_LIMIT)


def _dot(a, b):
    return jnp.dot(a, b, preferred_element_type=F32)


def _dot_nt(a, b):
    return lax.dot_general(a, b, (((1,), (1,)), ((), ())), preferred_element_type=F32)


def _split3(x):
    hi = x.astype(BF16)
    r = x - hi.astype(F32)
    mid = r.astype(BF16)
    lo = (r - mid.astype(F32)).astype(BF16)
    return hi, mid, lo


def _dot_exact_lhs(t, x, terms=3):
    parts = _split3(x)[:terms]
    out = _dot(t, parts[0])
    for part in parts[1:]:
        out = out + _dot(t, part)
    return out


def _dot_exact_rhs(x, t):
    hi, mid, lo = _split3(x)
    return _dot(hi, t) + _dot(mid, t) + _dot(lo, t)


def _sigmoid(x):
    return 1.0 / (1.0 + jnp.exp(-x))


def _silu(x):
    return x * _sigmoid(x)


def _log_sigmoid(x):
    return jnp.minimum(x, 0.0) - jnp.log(1.0 + jnp.exp(-jnp.abs(x)))


def _ada_body(c_ref, w_ref, b_ref, o_ref):
    c = c_ref[...]
    w = w_ref[0]
    c_hi = c.astype(BF16)
    c_lo = (c - c_hi.astype(F32)).astype(BF16)
    w_hi = w.astype(BF16)
    w_lo = (w - w_hi.astype(F32)).astype(BF16)
    o_ref[0] = _dot(c_hi, w_hi) + _dot(c_hi, w_lo) + _dot(c_lo, w_hi) + b_ref[0]


def _ada_mod(c_all, w_ada, b_ada):
    depth, d, d3 = w_ada.shape
    rows = c_all.shape[0]
    return pl.pallas_call(
        _ada_body,
        grid=(depth, d3 // d),
        in_specs=[pl.BlockSpec((rows, d), lambda l, j: (0, 0)),
                  pl.BlockSpec((1, d, d), lambda l, j: (l, 0, j)),
                  pl.BlockSpec((1, 1, d), lambda l, j: (l, 0, j))],
        out_specs=pl.BlockSpec((1, rows, d), lambda l, j: (l, 0, j)),
        out_shape=jax.ShapeDtypeStruct((depth, rows, d3), F32),
        compiler_params=_cparams(("arbitrary", "arbitrary")),
        name="ada_mod",
    )(c_all, w_ada, b_ada.reshape(depth, 1, d3))


def _mod_spec(mod, tm, seq):
    if mod.ndim == 3:
        return pl.BlockSpec((None, 1, mod.shape[-1]), lambda i: ((i * tm) // seq, 0, 0))
    return pl.BlockSpec((tm, mod.shape[-1]), lambda i: (i, 0))


def _mod_arrays(mod_l, batch, seq, tm, d):
    parts = [mod_l[:, k * d:(k + 1) * d] for k in range(3)]
    if seq % tm == 0:
        return [p.reshape(batch, 1, d) for p in parts]
    return [jnp.broadcast_to(p[:, None, :], (batch, seq, d)).reshape(batch * seq, d) for p in parts]


def _norm_body(x_ref, g_ref, sh_ref, sc_ref, u_ref):
    x = x_ref[...]
    ms = jnp.mean(x * x, axis=-1, keepdims=True)
    y = x * lax.rsqrt(ms + NORM_EPS) * g_ref[...]
    u_ref[...] = (y * (1.0 + sc_ref[...]) + sh_ref[...]).astype(BF16)


def _norm_mod(x2, g, shift, scale, tm, seq):
    m, d = x2.shape
    return pl.pallas_call(
        _norm_body,
        grid=(m // tm,),
        in_specs=[pl.BlockSpec((tm, d), lambda i: (i, 0)),
                  pl.BlockSpec((1, d), lambda i: (0, 0)),
                  _mod_spec(shift, tm, seq), _mod_spec(scale, tm, seq)],
        out_specs=pl.BlockSpec((tm, d), lambda i: (i, 0)),
        out_shape=jax.ShapeDtypeStruct((m, d), BF16),
        compiler_params=_cparams(("parallel",)),
        name="norm_mod",
    )(x2, g.reshape(1, d), shift, scale)


def _proj_hgrn_body(layer, u_ref, wf_ref, wq_ref, wi_ref, wz_ref, lbl_ref,
                    logf_ref, k_ref, q_ref, v_ref, z_ref):
    u = u_ref[...]
    lbl = lbl_ref[...]
    e = jnp.exp(lbl - jnp.max(lbl, axis=0, keepdims=True))
    sm = e / jnp.sum(e, axis=0, keepdims=True)
    lb = jnp.zeros_like(sm[0:1])
    for r in range(1, layer + 1):
        lb = lb + sm[r:r + 1]
    fl = _dot(u, wf_ref[...])
    logf_ref[...] = jnp.log(lb + (1.0 - lb) * _sigmoid(fl))
    k_ref[...] = (1.0 - lb) * _sigmoid(-fl)
    q_ref[...] = _silu(_dot(u, wq_ref[...])) * (DK_A ** -0.5)
    v_ref[...] = _dot(u, wi_ref[...]).astype(BF16)
    z_ref[...] = _silu(_dot(u, wz_ref[...])).astype(BF16)


def _proj_hgrn(u, wf, wq, wi, wz, lb_logits, layer, tm, tn=256):
    m, d = u.shape
    n = wf.shape[1]
    depth = lb_logits.shape[0]
    wspec = pl.BlockSpec((d, tn), lambda j, i: (0, j))
    ospec = pl.BlockSpec((tm, tn), lambda j, i: (i, j))
    return pl.pallas_call(
        functools.partial(_proj_hgrn_body, layer),
        grid=(n // tn, m // tm),
        in_specs=[pl.BlockSpec((tm, d), lambda j, i: (i, 0)), wspec, wspec, wspec, wspec,
                  pl.BlockSpec((depth, tn), lambda j, i: (0, j))],
        out_specs=[ospec] * 5,
        out_shape=[jax.ShapeDtypeStruct((m, n), F32), jax.ShapeDtypeStruct((m, n), F32),
                   jax.ShapeDtypeStruct((m, n), F32), jax.ShapeDtypeStruct((m, n), BF16),
                   jax.ShapeDtypeStruct((m, n), BF16)],
        compiler_params=_cparams(("parallel", "parallel")),
        name="proj_hgrn",
    )(u, wf, wq, wi, wz, lb_logits)


def _rope(x, cos, sin_signed):
    return x * cos + pltpu.roll(x, DK_B // 2, axis=1) * sin_signed


def _proj_ret_body(u_ref, wq_ref, wk_ref, wv_ref, wz_ref, cos_ref, sin_ref,
                   q_ref, k_ref, v_ref, z_ref):
    u = u_ref[...]
    cos = cos_ref[...]
    sin = sin_ref[...]
    q_ref[...] = _rope(_dot(u, wq_ref[...]), cos, sin) * (DK_B ** -0.5)
    k_ref[...] = _rope(_dot(u, wk_ref[...]), cos, sin)
    v_ref[...] = _dot(u, wv_ref[...]).astype(BF16)
    z_ref[...] = _silu(_dot(u, wz_ref[...])).astype(BF16)


def _proj_ret(u, wq, wk, wv, wz, cos, sin, tm):
    m, d = u.shape
    n_rope_tiles = cos.shape[0] // tm
    qk_w = pl.BlockSpec((d, DK_B), lambda h, i: (0, h))
    vz_w = pl.BlockSpec((d, DV_B), lambda h, i: (0, h))
    rope_spec = pl.BlockSpec((tm, DK_B), lambda h, i: (i % n_rope_tiles, 0))
    qk_o = pl.BlockSpec((tm, DK_B), lambda h, i: (i, h))
    vz_o = pl.BlockSpec((tm, DV_B), lambda h, i: (i, h))
    return pl.pallas_call(
        _proj_ret_body,
        grid=(H_B, m // tm),
        in_specs=[pl.BlockSpec((tm, d), lambda h, i: (i, 0)), qk_w, qk_w, vz_w, vz_w,
                  rope_spec, rope_spec],
        out_specs=[qk_o, qk_o, vz_o, vz_o],
        out_shape=[jax.ShapeDtypeStruct((m, H_B * DK_B), F32), jax.ShapeDtypeStruct((m, H_B * DK_B), F32),
                   jax.ShapeDtypeStruct((m, W_B), BF16), jax.ShapeDtypeStruct((m, W_B), BF16)],
        compiler_params=_cparams(("parallel", "parallel")),
        name="proj_ret",
    )(u, wq, wk, wv, wz, cos, sin)


def _proj_fox_body(u_ref, wq_ref, wk_ref, wv_ref, wz_ref,
                   q16_ref, k32_ref, k16_ref, v32_ref, v16_ref, z_ref):
    u = u_ref[...]
    q16_ref[...] = _dot(u, wq_ref[...]).astype(BF16)
    k = _dot(u, wk_ref[...])
    k32_ref[...] = k
    k16_ref[...] = k.astype(BF16)
    v = _dot(u, wv_ref[...])
    v32_ref[...] = v
    v16_ref[...] = v.astype(BF16)
    z_ref[...] = _silu(_dot(u, wz_ref[...])).astype(BF16)


def _proj_fox(u, wq, wk, wv, wz, tm, tn=256):
    m, d = u.shape
    n = wq.shape[1]
    wspec = pl.BlockSpec((d, tn), lambda j, i: (0, j))
    ospec = pl.BlockSpec((tm, tn), lambda j, i: (i, j))
    shp = lambda dt: jax.ShapeDtypeStruct((m, n), dt)
    return pl.pallas_call(
        _proj_fox_body,
        grid=(n // tn, m // tm),
        in_specs=[pl.BlockSpec((tm, d), lambda j, i: (i, 0)), wspec, wspec, wspec, wspec],
        out_specs=[ospec] * 6,
        out_shape=[shp(BF16), shp(F32), shp(BF16), shp(F32), shp(BF16), shp(BF16)],
        compiler_params=_cparams(("parallel", "parallel")),
        name="proj_fox",
    )(u, wq, wk, wv, wz)


def _proj_foxf_body(u_ref, whi_ref, wlo_ref, b_ref, o_ref):
    u = u_ref[...]
    cf = _dot(u, whi_ref[...]) + _dot(u, wlo_ref[...]) + b_ref[...]
    o_ref[...] = _log_sigmoid(cf)[:, :H_C]


def _proj_foxf(u, wf, bf, tm):
    m, d = u.shape
    wpad = jnp.pad(wf, ((0, 0), (0, LANES - H_C)))
    whi = wpad.astype(BF16)
    wlo = (wpad - whi.astype(F32)).astype(BF16)
    bpad = jnp.pad(bf.reshape(1, H_C), ((0, 0), (0, LANES - H_C)))
    wspec = pl.BlockSpec((d, LANES), lambda i: (0, 0))
    return pl.pallas_call(
        _proj_foxf_body,
        grid=(m // tm,),
        in_specs=[pl.BlockSpec((tm, d), lambda i: (i, 0)), wspec, wspec,
                  pl.BlockSpec((1, LANES), lambda i: (0, 0))],
        out_specs=pl.BlockSpec((tm, H_C), lambda i: (i, 0)),
        out_shape=jax.ShapeDtypeStruct((m, H_C), F32),
        compiler_params=_cparams(("parallel",)),
        name="proj_foxf",
    )(u, whi, wlo, bpad)


def _proj_gate_body(u_ref, w_ref, o_ref):
    o_ref[...] = _sigmoid(_dot(u_ref[...], w_ref[...])).astype(BF16)


def _proj_gate(u, wg, tm, tn=512):
    m, d = u.shape
    n = wg.shape[1]
    return pl.pallas_call(
        _proj_gate_body,
        grid=(n // tn, m // tm),
        in_specs=[pl.BlockSpec((tm, d), lambda j, i: (i, 0)),
                  pl.BlockSpec((d, tn), lambda j, i: (0, j))],
        out_specs=pl.BlockSpec((tm, tn), lambda j, i: (i, j)),
        out_shape=jax.ShapeDtypeStruct((m, n), BF16),
        compiler_params=_cparams(("parallel", "parallel")),
        name="proj_gate",
    )(u, wg)


def _hgrn_levels(c, n_real):
    return [c >> (i + 1) for i in range(int(math.log2(c))) if (c >> (i + 1)) < n_real]


def _chunk_rows(ref, cols, c):
    x = ref[:, cols]
    if x.shape[0] < c:
        x = jnp.concatenate([x, jnp.zeros((c - x.shape[0], x.shape[1]), x.dtype)], axis=0)
    return x


def _hgrn_tables(c, n_real):
    t = np.arange(c)[:, None]
    u = np.arange(c)[None, :]
    mats, masks = [], []
    for w in _hgrn_levels(c, n_real):
        r = (t // (2 * w)) * (2 * w) + w
        upper = (t % (2 * w)) >= w
        mats.append(np.where(upper, (u >= r) & (u <= t), (u > t) & (u < r)))
        masks.append((t // (2 * w)) == (u // (2 * w)))
    mats.append(u <= t)
    mats.append(u > t)
    masks.append(t == u)
    return (jnp.asarray(np.concatenate(mats, 0).astype(np.float32), BF16),
            jnp.asarray(np.stack(masks, 0).astype(np.float32), F32))


def _hgrn_body(c, nc, hp, n_real, logf_ref, k_ref, q_ref, v_ref, z_ref, g_ref, s0_ref, t_ref, mask_ref,
               y_ref, sout_ref, st_ref):
    ci = pl.program_id(2)

    @pl.when(ci == 0)
    def _():
        for hh in range(hp):
            st_ref[hh] = s0_ref[0, hh].T

    levels = _hgrn_levels(c, n_real)
    nl = len(levels)
    logf = _chunk_rows(logf_ref, slice(None), c)
    e_all = jnp.exp(_dot_exact_lhs(t_ref[...], logf, terms=2))
    row = lax.broadcasted_iota(jnp.int32, (c, DK_A), 0)
    g = g_ref[...]
    n_out = y_ref.shape[0]
    for hh in range(hp):
        cols = slice(hh * DK_A, (hh + 1) * DK_A)
        q = _chunk_rows(q_ref, cols, c)
        k = _chunk_rows(k_ref, cols, c)
        v = _chunk_rows(v_ref, cols, c).astype(BF16)
        a = mask_ref[nl] * _dot_nt(q.astype(BF16), k.astype(BF16))
        for li, w in enumerate(levels):
            e_w = e_all[li * c:(li + 1) * c, cols]
            upper = (row & w) != 0
            q_w = jnp.where(upper, q * e_w, 0.0).astype(BF16)
            k_w = jnp.where(upper, 0.0, k * e_w).astype(BF16)
            a = a + mask_ref[li] * _dot_nt(q_w, k_w)
        e_b = e_all[nl * c:(nl + 1) * c, cols]
        e_end = e_all[(nl + 1) * c:(nl + 2) * c, cols]
        st = st_ref[hh]
        o = _dot(a.astype(BF16), v) + _dot_nt((q * e_b).astype(BF16), st.astype(BF16))
        v_t = v.astype(F32).T.astype(BF16)
        st_ref[hh] = st * e_b[c - 1:c, :] + _dot(v_t, (k * e_end).astype(BF16))
        ms = jnp.mean(o * o, axis=-1, keepdims=True)
        y = o * lax.rsqrt(ms + NORM_EPS) * g * _chunk_rows(z_ref, cols, c)
        y_ref[:, cols] = y[:n_out].astype(y_ref.dtype)

    @pl.when(ci == nc - 1)
    def _():
        for hh in range(hp):
            sout_ref[0, hh] = st_ref[hh].T


def _hgrn_scan(logf, k, q, v, z, g, s0, batch, seq):
    c = HGRN_CHUNK
    hp = HGRN_HEADS_PER_STEP
    rows = min(seq, c)
    nc = seq // rows
    m = batch * seq
    tmat, masks = _hgrn_tables(c, rows)
    tok = pl.BlockSpec((rows, hp * DK_A), lambda b, h, ci: (b * nc + ci, h))
    st_spec = pl.BlockSpec((1, hp, DK_A, DV_A), lambda b, h, ci: (b, h, 0, 0))
    return pl.pallas_call(
        functools.partial(_hgrn_body, c, nc, hp, rows),
        grid=(batch, H_A // hp, nc),
        in_specs=[tok, tok, tok, tok, tok,
                  pl.BlockSpec((1, DV_A), lambda b, h, ci: (0, 0)),
                  st_spec,
                  pl.BlockSpec(tmat.shape, lambda b, h, ci: (0, 0)),
                  pl.BlockSpec(masks.shape, lambda b, h, ci: (0, 0, 0))],
        out_specs=[tok, st_spec],
        out_shape=[jax.ShapeDtypeStruct((m, W_A), BF16 if rows == c else F32),
                   jax.ShapeDtypeStruct((batch, H_A, DK_A, DV_A), F32)],
        scratch_shapes=[pltpu.VMEM((hp, DV_A, DK_A), F32)],
        compiler_params=_cparams(("parallel", "parallel", "arbitrary")),
        name="hgrn_scan",
    )(logf, k, q, v, z, g.reshape(1, DV_A), s0, tmat, masks)


def _ret_body(c, c_eff, nc, lg_ref, q_ref, k_ref, v_ref, z_ref, g_ref, s0_ref,
              y_ref, sout_ref, s_ref):
    ci = pl.program_id(1)

    @pl.when(ci == 0)
    def _():
        s_ref[...] = s0_ref[0]

    t = lax.broadcasted_iota(jnp.int32, (c, DK_B), 0).astype(F32)
    rel = (lax.broadcasted_iota(jnp.int32, (c, c), 0)
           - lax.broadcasted_iota(jnp.int32, (c, c), 1)).astype(F32)
    g = g_ref[...]
    for h in range(H_B):
        qk_cols = slice(h * DK_B, (h + 1) * DK_B)
        vz_cols = slice(h * DV_B, (h + 1) * DV_B)
        lg = lg_ref[h]
        q = _chunk_rows(q_ref, qk_cols, c)
        k = _chunk_rows(k_ref, qk_cols, c)
        v = _chunk_rows(v_ref, vz_cols, c).astype(BF16)
        q_dec = jnp.exp((t + 1.0) * lg)
        k_dec = jnp.where(t < c_eff, jnp.exp(jnp.maximum(c_eff - 1.0 - t, 0.0) * lg), 0.0)
        intra = jnp.where(rel >= 0.0, jnp.exp(jnp.maximum(rel, 0.0) * lg), 0.0)
        a = _dot_nt(q.astype(BF16), k.astype(BF16)) * intra
        s = s_ref[h]
        o = _dot(a.astype(BF16), v) + _dot((q * q_dec).astype(BF16), s.astype(BF16))
        kd_t = (k * k_dec).T.astype(BF16)
        chunk_dec = jnp.exp(jnp.full((1, DV_B), c_eff, F32) * lg)
        s_ref[h] = chunk_dec * s + _dot(kd_t, v)
        ms = jnp.mean(o * o, axis=-1, keepdims=True)
        y = o * lax.rsqrt(ms + NORM_EPS) * g * _chunk_rows(z_ref, vz_cols, c)
        y_ref[:, vz_cols] = y[:y_ref.shape[0]].astype(y_ref.dtype)

    @pl.when(ci == nc - 1)
    def _():
        sout_ref[0] = s_ref[...]


def _ret_scan(q, k, v, z, g, s0, batch, seq):
    c = RET_CHUNK if seq >= RET_CHUNK else SHORT_CHUNK
    c_eff = min(seq, c)
    nc = seq // c_eff
    m = batch * seq
    log_gamma = jnp.log1p(-jnp.power(2.0, -RET_DECAY_BASE - jnp.arange(H_B, dtype=F32)))
    qk = pl.BlockSpec((c_eff, H_B * DK_B), lambda b, ci: (b * nc + ci, 0))
    vz = pl.BlockSpec((c_eff, W_B), lambda b, ci: (b * nc + ci, 0))
    st_spec = pl.BlockSpec((1, H_B, DK_B, DV_B), lambda b, ci: (b, 0, 0, 0))
    return pl.pallas_call(
        functools.partial(_ret_body, c, c_eff, nc),
        grid=(batch, nc),
        in_specs=[pl.BlockSpec(memory_space=pltpu.SMEM), qk, qk, vz, vz,
                  pl.BlockSpec((1, DV_B), lambda b, ci: (0, 0)), st_spec],
        out_specs=[vz, st_spec],
        out_shape=[jax.ShapeDtypeStruct((m, W_B), BF16 if c_eff == c else F32),
                   jax.ShapeDtypeStruct((batch, H_B, DK_B, DV_B), F32)],
        scratch_shapes=[pltpu.VMEM((H_B, DK_B, DV_B), F32)],
        compiler_params=_cparams(("parallel", "arbitrary")),
        name="ret_scan",
    )(log_gamma, q, k, v, z, g.reshape(1, DV_B), s0)


def _fcum_body(nblk, x_ref, o_ref):
    r = lax.broadcasted_iota(jnp.int32, (LANES, LANES), 0)
    cc = lax.broadcasted_iota(jnp.int32, (LANES, LANES), 1)
    tri = jnp.where(r <= cc, 1.0, 0.0).astype(BF16)
    carry = jnp.zeros((H_C, 1), F32)
    for blk in range(nblk):
        x = x_ref[0, :, blk * LANES:(blk + 1) * LANES]
        y = _dot_exact_rhs(x, tri) + carry
        o_ref[0, :, blk * LANES:(blk + 1) * LANES] = y * LOG2E
        carry = y[:, LANES - 1:LANES]


def _fcum(logf_t):
    b, h, seq = logf_t.shape
    spec = pl.BlockSpec((1, h, seq), lambda i: (i, 0, 0))
    return pl.pallas_call(
        functools.partial(_fcum_body, seq // LANES),
        grid=(b,),
        in_specs=[spec], out_specs=spec,
        out_shape=jax.ShapeDtypeStruct((b, h, seq), F32),
        compiler_params=_cparams(("parallel",)),
        name="fox_fcum",
    )(logf_t)


def _fold_lanes(x, op):
    out = x[:, 0:LANES]
    for c in range(1, x.shape[1] // LANES):
        out = op(out, x[:, c * LANES:(c + 1) * LANES])
    return out


def _fox_prompt_body(tq, q_ref, k_ref, v_ref, fq_ref, fk_ref, z_ref, o_ref, fqc_ref, t_ref, p_ref, st_ref):
    i = pl.program_id(2)
    q = (q_ref[...].astype(F32) * (DH_C ** -0.5 * LOG2E)).astype(BF16)
    fqc_ref[...] = jnp.broadcast_to(fq_ref[0, 0], (LANES, tq)).T
    m_ref, l_ref, a_ref, acc_ref = st_ref.at[0], st_ref.at[1], st_ref.at[2], st_ref.at[3]
    m_ref[...] = jnp.full((tq, LANES), MASK_VALUE, F32)
    l_ref[...] = jnp.zeros((tq, LANES), F32)
    a_ref[...] = jnp.ones((tq, LANES), F32)
    acc_ref[...] = jnp.zeros((tq, DH_C), F32)

    def scores(j):
        start = pl.multiple_of(j * tq, tq)
        return _dot_nt(q, k_ref[pl.ds(start, tq), :]) - fk_ref[0, 0, :, pl.ds(start, tq)]

    def softmax(slot, diagonal):
        for r in range(tq // FOX_SOFTMAX_ROWS):
            rows = slice(r * FOX_SOFTMAX_ROWS, (r + 1) * FOX_SOFTMAX_ROWS)
            t = t_ref[slot, rows, :]
            if diagonal:
                rr = lax.broadcasted_iota(jnp.int32, t.shape, 0) + r * FOX_SOFTMAX_ROWS
                cc = lax.broadcasted_iota(jnp.int32, t.shape, 1)
                t = jnp.where(rr >= cc, t, MASK_VALUE)
            fq = fqc_ref[rows, :]
            m_prev = m_ref[rows, :]
            m_new = jnp.maximum(m_prev, jnp.max(_fold_lanes(t, jnp.maximum), axis=-1, keepdims=True) + fq)
            p = jnp.exp2(t + jnp.concatenate([fq - m_new] * (tq // LANES), axis=1))
            alpha = jnp.exp2(m_prev - m_new)
            l_ref[rows, :] = alpha * l_ref[rows, :] + jnp.sum(_fold_lanes(p, jnp.add), axis=-1, keepdims=True)
            a_ref[rows, :] = alpha
            m_ref[rows, :] = m_new
            p_ref[slot, rows, :] = p.astype(BF16)

    def weighted(j, slot):
        start = pl.multiple_of(j * tq, tq)
        acc_ref[...] = a_ref[...] * acc_ref[...] + _dot(p_ref[slot], v_ref[pl.ds(start, tq), :])

    t_ref[0] = scores(0)
    p_ref[1] = jnp.zeros((tq, tq), BF16)

    def trip(j, carry):
        slot = j % 2
        weighted(jnp.maximum(j - 1, 0), 1 - slot)
        softmax(slot, False)
        t_ref[1 - slot] = scores(j + 1)
        return carry

    lax.fori_loop(0, i, trip, 0)
    slot = i % 2
    weighted(jnp.maximum(i - 1, 0), 1 - slot)
    softmax(slot, True)
    weighted(i, slot)
    o_ref[...] = (acc_ref[...] / l_ref[...] * z_ref[...]).astype(BF16)


def _fox_prompt(q16, k16, v16, f_row, z, batch, seq):
    tq = min(FOX_BLOCK, seq)
    nq = seq // tq
    m = batch * seq
    qspec = pl.BlockSpec((tq, DH_C), lambda b, h, i: (b * nq + i, h))
    kvspec = pl.BlockSpec((seq, DH_C), lambda b, h, i: (b, h))
    return pl.pallas_call(
        functools.partial(_fox_prompt_body, tq),
        grid=(batch, H_C, nq),
        in_specs=[qspec, kvspec, kvspec,
                  pl.BlockSpec((1, 1, 1, tq), lambda b, h, i: (b, h, 0, i)),
                  pl.BlockSpec((1, 1, 1, seq), lambda b, h, i: (b, h, 0, 0)),
                  qspec],
        out_specs=qspec,
        out_shape=jax.ShapeDtypeStruct((m, W_C), BF16),
        scratch_shapes=[pltpu.VMEM((tq, LANES), F32), pltpu.VMEM((2, tq, tq), F32),
                        pltpu.VMEM((2, tq, tq), BF16), pltpu.VMEM((4, tq, LANES), F32)],
        compiler_params=_cparams(("parallel", "parallel", "arbitrary")),
        name="fox_prompt",
    )(q16, k16, v16, f_row, f_row, z)


def _tri_f32(n, kind):
    r = lax.broadcasted_iota(jnp.int32, (n, n), 0)
    c = lax.broadcasted_iota(jnp.int32, (n, n), 1)
    cond = (c > r) if kind == "after" else (c <= r)
    return jnp.where(cond, 1.0, 0.0).astype(BF16)


def _new_bias_body(n, x_ref, o_ref):
    o_ref[0] = -_dot_exact_lhs(_tri_f32(n, "upto"), x_ref[0])


def _new_bias(logf_new_pad):
    nb, n, h = logf_new_pad.shape
    spec = pl.BlockSpec((1, n, h), lambda b: (b, 0, 0))
    return pl.pallas_call(
        functools.partial(_new_bias_body, n),
        grid=(nb,),
        in_specs=[spec], out_specs=spec,
        out_shape=jax.ShapeDtypeStruct((nb, n, h), F32),
        compiler_params=_cparams(("parallel",)),
        name="fox_new_bias",
    )(logf_new_pad)


def _fox_decode_body(n_groups, pg, ps, n_new, new_pad, layer, n_pages, pt_ref, q_ref, k_hbm, v_hbm, lf_hbm,
                     rep_ref, kn_ref, vn_ref, hn_ref, z_ref, o_ref, kbuf, vbuf, lfbuf, sem):
    b = pl.program_id(0)
    scale = DH_C ** -0.5
    rows = H_C * n_new
    cols = ps * H_C
    n_slots = kbuf.shape[0]

    def group_copies(g, slot):
        out = []
        for i in range(pg):
            page = pt_ref[b, n_pages - 1 - (g * pg + i)]
            out.append(pltpu.make_async_copy(k_hbm.at[page, layer], kbuf.at[slot, i], sem.at[0, slot]))
            out.append(pltpu.make_async_copy(v_hbm.at[page, layer], vbuf.at[slot, i], sem.at[1, slot]))
            out.append(pltpu.make_async_copy(lf_hbm.at[layer, page], lfbuf.at[slot, i], sem.at[2, slot]))
        return out

    for g0 in range(min(n_slots - 1, n_groups)):
        for cp in group_copies(g0, g0):
            cp.start()

    def group(g, state):
        m_prev, l_prev, acc_prev, carry = state
        slot = g % n_slots
        ahead = g + n_slots - 1

        @pl.when(ahead < n_groups)
        def _():
            for cp in group_copies(ahead, ahead % n_slots):
                cp.start()

        for cp in group_copies(g, slot):
            cp.wait()

        lf = jnp.concatenate([lfbuf[slot, i] for i in range(pg)], axis=0)
        t_from = lax.broadcasted_iota(jnp.int32, (ps, ps), 0)
        t_to = lax.broadcasted_iota(jnp.int32, (ps, ps), 1)
        later = jnp.where(t_from > t_to, 1.0, 0.0).astype(BF16)
        within = _dot_exact_rhs(lf, later)
        total = jnp.sum(lf, axis=-1, keepdims=True)
        bias = []
        for i in range(pg):
            bias.append(within[i * H_C:(i + 1) * H_C] + carry)
            carry = carry + total[i * H_C:(i + 1) * H_C]
        spread = _dot_exact_rhs(jnp.concatenate(bias, axis=0), rep_ref[...])
        other_head = jnp.where(lax.broadcasted_iota(jnp.int32, (H_C, cols), 0)
                               == lax.broadcasted_iota(jnp.int32, (H_C, cols), 1) % H_C, 0.0, MASK_VALUE)

        q = q_ref[0]
        s_list = []
        for i in range(pg):
            kf = kbuf[slot, i].reshape(cols, DH_C).astype(BF16)
            b8 = spread[i * H_C:(i + 1) * H_C] + other_head
            s_list.append(_dot_nt(q, kf) * scale + jnp.concatenate([b8] * n_new, axis=0))
        mx = s_list[0]
        for s in s_list[1:]:
            mx = jnp.maximum(mx, s)
        m_new = jnp.maximum(m_prev, jnp.max(_fold_lanes(mx, jnp.maximum), axis=-1, keepdims=True))
        alpha = jnp.exp(m_prev - m_new)
        psum = None
        pv = None
        for i in range(pg):
            p = jnp.exp(s_list[i] - m_new)
            contrib = _dot(p.astype(BF16), vbuf[slot, i].reshape(cols, DH_C).astype(BF16))
            psum = p if psum is None else psum + p
            pv = contrib if pv is None else pv + contrib
        l_new = alpha * l_prev + jnp.sum(_fold_lanes(psum, jnp.add), axis=-1, keepdims=True)
        return m_new, l_new, alpha * acc_prev + pv, carry

    init = (jnp.full((rows, 1), MASK_VALUE, F32), jnp.zeros((rows, 1), F32), jnp.zeros((rows, DH_C), F32),
            jnp.zeros((H_C, 1), F32))
    m_prev, l_prev, acc_prev, _ = lax.fori_loop(0, n_groups, group, init)

    ncol = new_pad * H_C
    s = _dot_nt(q_ref[0], kn_ref[0]) * scale + hn_ref[0]
    rr = lax.broadcasted_iota(jnp.int32, (rows, ncol), 0)
    cc = lax.broadcasted_iota(jnp.int32, (rows, ncol), 1)
    tok = cc // H_C
    ok = ((rr % H_C) == (cc % H_C)) & (tok <= (rr // H_C)) & (tok < n_new)
    s = jnp.where(ok, s, MASK_VALUE)
    m_new = jnp.maximum(m_prev, jnp.max(s, axis=-1, keepdims=True))
    p = jnp.exp(s - m_new)
    alpha = jnp.exp(m_prev - m_new)
    l_fin = alpha * l_prev + jnp.sum(p, axis=-1, keepdims=True)
    acc = alpha * acc_prev + _dot(p.astype(BF16), vn_ref[0])
    o_ref[0] = (acc / l_fin * z_ref[0]).astype(BF16)


def _fox_decode(q_rows, cache_k, cache_v, lf_t, k_new, v_new, hn_rows, z_rows, page_table, layer,
                n_new, new_pad):
    nb, n_pages = page_table.shape
    ps = cache_k.shape[2]
    pg = min(DECODE_PAGES_PER_GROUP, n_pages)
    n_groups = n_pages // pg
    rows = H_C * n_new
    cols = ps * H_C
    ncol = new_pad * H_C
    rep = jnp.asarray((np.arange(ps)[:, None] == np.arange(cols)[None, :] // H_C).astype(np.float32), BF16)

    hbm = pl.BlockSpec(memory_space=pltpu.HBM)
    per_b = lambda shape: pl.BlockSpec((1,) + shape, lambda b, pt: (b, 0, 0))
    grid_spec = pltpu.PrefetchScalarGridSpec(
        num_scalar_prefetch=1,
        grid=(nb,),
        in_specs=[per_b((rows, DH_C)), hbm, hbm, hbm,
                  pl.BlockSpec((ps, cols), lambda b, pt: (0, 0)),
                  per_b((ncol, DH_C)), per_b((ncol, DH_C)), per_b((1, ncol)), per_b((rows, DH_C))],
        out_specs=per_b((rows, DH_C)),
        scratch_shapes=[pltpu.VMEM((DECODE_SLOTS, pg, ps, H_C, DH_C), F32),
                        pltpu.VMEM((DECODE_SLOTS, pg, ps, H_C, DH_C), F32),
                        pltpu.VMEM((DECODE_SLOTS, pg, H_C, ps), F32),
                        pltpu.SemaphoreType.DMA((3, DECODE_SLOTS))],
    )
    return pl.pallas_call(
        functools.partial(_fox_decode_body, n_groups, pg, ps, n_new, new_pad, layer, n_pages),
        grid_spec=grid_spec,
        out_shape=jax.ShapeDtypeStruct((nb, rows, DH_C), BF16),
        compiler_params=_cparams(("arbitrary",)),
        name="fox_decode",
    )(page_table, q_rows, cache_k, cache_v, lf_t, rep, k_new, v_new, hn_rows, z_rows)


def _merge_body(final, ya_ref, yb_ref, yc_ref, g0_ref, g1_ref, g2_ref, x_ref, gate_ref,
                wa_ref, wb_ref, wc_ref, wo_ref, gf_ref, xo_ref, *maybe_y):
    merged = (g0_ref[...] * _dot(ya_ref[...], wa_ref[...])
              + g1_ref[...] * _dot(yb_ref[...], wb_ref[...])
              + g2_ref[...] * _dot(yc_ref[...], wc_ref[...]))
    xn = x_ref[...] + gate_ref[...] * _dot(merged.astype(BF16), wo_ref[...])
    xo_ref[...] = xn
    if final:
        ms = jnp.mean(xn * xn, axis=-1, keepdims=True)
        maybe_y[0][...] = xn * lax.rsqrt(ms + NORM_EPS) * gf_ref[...]


def _merge(ya, yb, yc, g, x2, gate, wa, wb, wc, wo, g_final, tm, seq, final):
    m, d = x2.shape
    row = pl.BlockSpec((tm, d), lambda i: (i, 0))
    gspec = lambda k: pl.BlockSpec((tm, d), lambda i: (i, k))
    wspec = pl.BlockSpec((d, d), lambda i: (0, 0))
    out_shape = [jax.ShapeDtypeStruct((m, d), F32)]
    out_specs = [row]
    if final:
        out_shape.append(jax.ShapeDtypeStruct((m, d), F32))
        out_specs.append(row)
    return pl.pallas_call(
        functools.partial(_merge_body, final),
        grid=(m // tm,),
        in_specs=[row, row, row, gspec(0), gspec(1), gspec(2), row, _mod_spec(gate, tm, seq),
                  wspec, wspec, wspec, wspec, pl.BlockSpec((1, d), lambda i: (0, 0))],
        out_specs=out_specs,
        out_shape=out_shape,
        compiler_params=_cparams(("parallel",)),
        name="merge_out",
    )(ya, yb, yc, g, g, g, x2, gate, wa, wb, wc, wo, g_final.reshape(1, d))


def _col_offsets(d):
    sizes = (H_A * DK_A, H_A * DK_A, W_A, W_A,
             H_B * DK_B, H_B * DK_B, W_B, W_B,
             W_C, W_C, W_C, H_C, W_C,
             N_BRANCH * d)
    offs = np.concatenate([[0], np.cumsum(sizes)])
    return [(int(offs[i]), int(offs[i + 1])) for i in range(len(sizes))]


def _rope_tables(pos):
    half = DK_B // 2
    inv = ROPE_BASE ** (-jnp.arange(half, dtype=F32) / half)
    ang = pos.astype(F32)[:, None] * inv[None, :]
    cos = jnp.cos(ang)
    sin = jnp.sin(ang)
    return jnp.concatenate([cos, cos], axis=-1), jnp.concatenate([-sin, sin], axis=-1)


def _pad_tokens(a, batch, seq, seq_pad):
    n = a.shape[-1]
    return jnp.pad(a.reshape(batch, seq, n), ((0, 0), (0, seq_pad - seq), (0, 0))).reshape(batch * seq_pad, n)


def _layer(layer, x2, batch, seq, mod_l, pos, w, s_hgrn, s_ret, past, final):
    m, d = x2.shape
    tm = min(512, m)
    shift, scale, gate = _mod_arrays(mod_l, batch, seq, tm, d)
    u = _norm_mod(x2, w["g_norm"], shift, scale, tm, seq)

    cols = _col_offsets(d)
    w_in = w["w_in"]
    wcol = lambda i: w_in[:, cols[i][0]:cols[i][1]].astype(BF16)

    tp = min(PROJ_ROWS, m)
    logf_a, k_a, q_a, v_a, z_a = _proj_hgrn(u, wcol(0), wcol(1), wcol(2), wcol(3), w["lb_logits"], layer, tp)
    cos, sin = _rope_tables(pos)
    if cos.shape[0] % tp != 0:
        cos = jnp.tile(cos, (m // cos.shape[0], 1))
        sin = jnp.tile(sin, (m // sin.shape[0], 1))
    q_b, k_b, v_b, z_b = _proj_ret(u, wcol(4), wcol(5), wcol(6), wcol(7), cos, sin, tp)
    q_c, k_c32, k_c16, v_c32, v_c16, z_c = _proj_fox(u, wcol(8), wcol(9), wcol(10), wcol(12), tp)
    logf_c = _proj_foxf(u, w_in[:, cols[11][0]:cols[11][1]], w["b_fox_f"], tp)
    gates = _proj_gate(u, wcol(13), tp)

    if past is None:
        y_a, s_hgrn_new = _hgrn_scan(logf_a, k_a, q_a, v_a, z_a, w["g_hgrn"], s_hgrn, batch, seq)
        y_b, s_ret_new = _ret_scan(q_b, k_b, v_b, z_b, w["g_ret"], s_ret, batch, seq)
        f_row = _fcum(logf_c.reshape(batch, seq, H_C).transpose(0, 2, 1)).reshape(batch, H_C, 1, seq)
        y_c = _fox_prompt(q_c, k_c16, v_c16, f_row, z_c, batch, seq)
    else:
        cache_k, cache_v, lf_t, page_table = past
        f32 = lambda a: a.astype(F32)
        y_a, s_hgrn_new = _hgrn_scan(logf_a, k_a, q_a, f32(v_a), f32(z_a), w["g_hgrn"], s_hgrn, batch, seq)
        y_b, s_ret_new = _ret_scan(q_b, k_b, f32(v_b), f32(z_b), w["g_ret"], s_ret, batch, seq)
        y_a, y_b = y_a.astype(BF16), y_b.astype(BF16)

        ps = cache_k.shape[2]
        new_pad = LANES // H_C
        hn = _new_bias(_pad_tokens(logf_c, batch, seq, ps).reshape(batch, ps, H_C))
        hn_rows = hn[:, :new_pad].reshape(batch, 1, new_pad * H_C)
        to_rows = lambda a: a.reshape(batch, seq * H_C, DH_C)
        new_rows = lambda a: _pad_tokens(a, batch, seq, new_pad).reshape(batch, new_pad * H_C, DH_C)
        y_rows = _fox_decode(to_rows(q_c), cache_k, cache_v, lf_t, new_rows(k_c16), new_rows(v_c16),
                             hn_rows, to_rows(z_c), page_table, layer, seq, new_pad)
        y_c = y_rows.reshape(m, W_C)

    bf = lambda name: w[name].astype(BF16)
    outs = _merge(y_a, y_b, y_c, gates, x2, gate, bf("w_proj_a"), bf("w_proj_b"), bf("w_proj_c"),
                  bf("w_out"), w["g_final"], min(256, m), seq, final)
    return outs, s_hgrn_new, s_ret_new, k_c32, v_c32, logf_c


def kernel(x_prompt, x_sample, c_prompt, c_sample, cache_k, cache_v, cache_logf, state_hgrn, state_ret,
           page_table, w_ada, b_ada, g_norm, w_in, b_fox_f, lb_logits, g_hgrn, g_ret, w_proj_a, w_proj_b,
           w_proj_c, w_out, g_final):
    bp, lp, d = x_prompt.shape
    bs, ls, _ = x_sample.shape
    depth = w_in.shape[0]
    n_pages = page_table.shape[1]
    past_len = n_pages * cache_k.shape[2]
    pos_p = jnp.arange(lp, dtype=jnp.int32)
    pos_s = past_len + jnp.arange(ls, dtype=jnp.int32)

    n_c = bp + bs
    rows = -(-n_c // 8) * 8
    c_all = jnp.pad(jnp.concatenate([c_prompt, c_sample], axis=0), ((0, rows - n_c), (0, 0)))
    mod = _ada_mod(c_all, w_ada, b_ada)

    lf_t = jnp.transpose(cache_logf, (1, 0, 3, 2))
    zero_hgrn = jnp.zeros((bp, H_A, DK_A, DV_A), F32)
    zero_ret = jnp.zeros((bp, H_B, DK_B, DV_B), F32)
    xp = x_prompt.reshape(bp * lp, d)
    xs = x_sample.reshape(bs * ls, d)
    acc = {k: [] for k in ("kp", "vp", "fp", "hp", "rp", "ks", "vs", "fs", "hs", "rs")}
    yp = ys = None
    for l in range(depth):
        w = dict(g_norm=g_norm[l], w_in=w_in[l], b_fox_f=b_fox_f[l], lb_logits=lb_logits,
                 g_hgrn=g_hgrn[l], g_ret=g_ret[l], w_proj_a=w_proj_a[l], w_proj_b=w_proj_b[l],
                 w_proj_c=w_proj_c[l], w_out=w_out[l], g_final=g_final)
        final = l == depth - 1
        outs, h, r, k, v, f = _layer(l, xp, bp, lp, mod[l, :bp], pos_p, w, zero_hgrn, zero_ret, None, final)
        xp = outs[0]
        if final:
            yp = outs[1]
        acc["kp"].append(k.reshape(bp, lp, H_C, DH_C)); acc["vp"].append(v.reshape(bp, lp, H_C, DH_C))
        acc["fp"].append(f.reshape(bp, lp, H_C)); acc["hp"].append(h); acc["rp"].append(r)

        outs, h, r, k, v, f = _layer(l, xs, bs, ls, mod[l, bp:bp + bs], pos_s, w, state_hgrn[:, l],
                                     state_ret[:, l], (cache_k, cache_v, lf_t, page_table), final)
        xs = outs[0]
        if final:
            ys = outs[1]
        acc["ks"].append(k.reshape(bs, ls, H_C, DH_C)); acc["vs"].append(v.reshape(bs, ls, H_C, DH_C))
        acc["fs"].append(f.reshape(bs, ls, H_C)); acc["hs"].append(h); acc["rs"].append(r)

    st = lambda name: jnp.stack(acc[name], axis=1)
    return (yp.reshape(bp, lp, d), ys.reshape(bs, ls, d),
            st("kp"), st("vp"), st("fp"), st("hp"), st("rp"),
            st("ks"), st("vs"), st("fs"), st("hs"), st("rs"))
```
